```python
import math
import jax, jax.numpy as jnp
from jax import lax
import numpy as np

D_MODEL = 1024
BATCH = 8
SEQ = 2048
DEPTH = 1
DEC_BATCH = 16
DEC_SEQ = 64
PAST_LEN = 1024

CHUNK = 64
Q_BLOCK = 128
A_QK_DIM = 64
A_V_DIM = 2 * A_QK_DIM
A_HEADS = D_MODEL // 256
A_WIDTH = A_HEADS * A_V_DIM
B_DIM = 64
B_HEADS = D_MODEL // 128
B_WIDTH = B_HEADS * B_DIM
MIX_WIDTH = A_WIDTH + B_WIDTH
IN_SPLITS = [A_WIDTH, 2 * A_WIDTH, 3 * A_WIDTH, 3 * A_WIDTH + B_WIDTH, 3 * A_WIDTH + 2 * B_WIDTH]
IN_COLS = 3 * A_WIDTH + 3 * B_WIDTH
N_EXPERTS = 64
TOP_K = 6
N_GROUPS = 8
TOPK_GROUPS = 4
EXPERTS_PER_GROUP = N_EXPERTS // N_GROUPS
D_EXPERT = D_MODEL // 4
D_SHARED = D_MODEL // 4
ROUTED_SCALE = 2.5
EPS = 1e-6
NEG_INF = -1e30

kernel_name = "hymba_diffattn_stickbreak_moe_adaln_stream_step"


def rms(x):
    xf = x.astype(jnp.float32)
    y = xf * lax.rsqrt(jnp.mean(xf * xf, axis=-1, keepdims=True) + EPS)
    return y.astype(x.dtype)


def lambda_init_of(layer):
    return 0.8 - 0.6 * math.exp(-0.3 * layer)


def alibi_slopes():
    return jnp.exp2(-8.0 * (jnp.arange(A_HEADS, dtype=jnp.float32) + 1.0) / A_HEADS)


def diff_attention(q, k, v, qpos, kpos, lam):
    s = jnp.einsum('bqhmd,bkhmd->bmhqk', q, k).astype(jnp.float32) * (A_QK_DIM ** -0.5)
    dist = jnp.abs(qpos[:, None] - kpos[None, :]).astype(jnp.float32)
    bias = -alibi_slopes()[:, None, None] * dist
    visible = (kpos[None, :] // CHUNK) <= (qpos[:, None] // CHUNK)
    s = jnp.where(visible, s + bias, NEG_INF)
    p = jax.nn.softmax(s, axis=-1)
    w = p[:, 0] - lam * p[:, 1]
    return jnp.einsum('bhqk,bkhd->bqhd', w.astype(v.dtype), v)


def stick_breaking(q, k, v, qpos, kpos):
    z = jnp.einsum('bqhd,bkhd->bhqk', q, k).astype(jnp.float32) * (B_DIM ** -0.5)
    before = kpos[None, :] < qpos[:, None]
    log_keep = jnp.where(before, jax.nn.log_sigmoid(-z), 0.0)
    rev = lax.cumsum(log_keep, axis=3, reverse=True)
    after = jnp.concatenate([rev[..., 1:], jnp.zeros_like(rev[..., :1])], axis=-1)
    a = jnp.where(before, jnp.exp(jax.nn.log_sigmoid(z) + after), 0.0)
    return jnp.einsum('bhqk,bkhd->bqhd', a.astype(v.dtype), v)


def sweep_queries(fn, q, qpos):
    b, t = q.shape[:2]
    if t % Q_BLOCK != 0 or t <= Q_BLOCK:
        return fn(q, qpos)
    nb = t // Q_BLOCK
    qb = jnp.moveaxis(q.reshape((b, nb, Q_BLOCK) + q.shape[2:]), 1, 0)
    pb = qpos.reshape(nb, Q_BLOCK)
    out = lax.map(lambda blk: fn(blk[0], blk[1]), (qb, pb))
    out = jnp.moveaxis(out, 0, 1)
    return out.reshape((b, t) + out.shape[3:])


def moe_ffn(h, p):
    t = h.shape[0]
    scores = jax.nn.sigmoid((h @ p['w_router']).astype(jnp.float32))
    sel = scores + p['b_router'].astype(jnp.float32)
    group_score = lax.top_k(sel.reshape(t, N_GROUPS, EXPERTS_PER_GROUP), 2)[0].sum(-1)
    _, gidx = lax.top_k(group_score, TOPK_GROUPS)
    gkeep = jnp.any(gidx[..., None] == jnp.arange(N_GROUPS), axis=-2)
    sel = jnp.where(jnp.repeat(gkeep, EXPERTS_PER_GROUP, axis=-1), sel, NEG_INF)
    _, eidx = lax.top_k(sel, TOP_K)
    wts = jnp.take_along_axis(scores, eidx, axis=-1)
    wts = wts / jnp.sum(wts, axis=-1, keepdims=True) * ROUTED_SCALE
    comb = jnp.sum((eidx[..., None] == jnp.arange(N_EXPERTS)) * wts[..., None], axis=-2)
    g = jnp.einsum('td,edf->tef', h, p['w_gate'])
    u = jnp.einsum('td,edf->tef', h, p['w_up'])
    a = jax.nn.silu(g) * u * comb[..., None].astype(h.dtype)
    routed = jnp.einsum('tef,efd->td', a, p['w_down'])
    shared = (jax.nn.silu(h @ p['w_sg']) * (h @ p['w_su'])) @ p['w_sd']
    return routed + shared


def trunk_layer(x, c, past, layer, p):
    b, t, _ = x.shape
    mod = jax.nn.silu(c) @ p['w_ada'] + p['b_ada']
    sh1, sc1, g1, sh2, sc2, g2 = jnp.split(mod, 6, axis=-1)
    h = rms(x) * (1 + sc1[:, None]) + sh1[:, None]
    proj = h @ p['w_in']
    qa, ka, va, qb, kb, vb = jnp.split(proj, IN_SPLITS, axis=-1)
    ka = ka.reshape(b, t, A_HEADS, 2 * A_QK_DIM)
    va = va.reshape(b, t, A_HEADS, A_V_DIM)
    kb = kb.reshape(b, t, B_HEADS, B_DIM)
    vb = vb.reshape(b, t, B_HEADS, B_DIM)
    new_rows = (ka, va, kb, vb)
    if past is None:
        pos0 = 0
        ka_all, va_all, kb_all, vb_all = ka, va, kb, vb
    else:
        pk_a, pv_a, pk_b, pv_b = past
        pos0 = pk_a.shape[1]
        ka_all = jnp.concatenate([pk_a, ka], axis=1)
        va_all = jnp.concatenate([pv_a, va], axis=1)
        kb_all = jnp.concatenate([pk_b, kb], axis=1)
        vb_all = jnp.concatenate([pv_b, vb], axis=1)
    qpos = pos0 + jnp.arange(t, dtype=jnp.int32)
    kpos = jnp.arange(pos0 + t, dtype=jnp.int32)

    lam_init = lambda_init_of(layer)
    lp = p['lam'].astype(jnp.float32)
    lam = jnp.exp(jnp.sum(lp[0] * lp[1])) - jnp.exp(jnp.sum(lp[2] * lp[3])) + lam_init
    qa5 = qa.reshape(b, t, A_HEADS, 2, A_QK_DIM)
    ka5 = ka_all.reshape(b, pos0 + t, A_HEADS, 2, A_QK_DIM)
    oa = sweep_queries(lambda qq, pp: diff_attention(qq, ka5, va_all, pp, kpos, lam), qa5, qpos)
    oa = rms(oa) * p['g_sub_a'] * (1.0 - lam_init)

    qb4 = qb.reshape(b, t, B_HEADS, B_DIM)
    ob = sweep_queries(lambda qq, pp: stick_breaking(qq, kb_all, vb_all, pp, kpos), qb4, qpos)
    ob = rms(ob) * p['g_sub_b']

    o = jnp.concatenate([oa.reshape(b, t, A_WIDTH), ob.reshape(b, t, B_WIDTH)], axis=-1) @ p['w_out']
    x = x + g1[:, None] * o

    h2 = rms(x) * (1 + sc2[:, None]) + sh2[:, None]
    x = x + g2[:, None] * lax.map(lambda hs: moe_ffn(hs, p), h2)
    return x, new_rows


def setup_inputs(seed: int = 0) -> dict:
    key = jax.random.key(seed)
    ks = jax.random.split(key, 26)
    f32 = jnp.float32
    nrm = lambda k, shape, s: jax.random.normal(k, shape, f32) * s
    d = D_MODEL
    return {
        'x_prompt': nrm(ks[0], (BATCH, SEQ, d), 1.0),
        'x_sample': nrm(ks[1], (DEC_BATCH, DEC_SEQ, d), 1.0),
        'cache_a_k': nrm(ks[2], (DEPTH, DEC_BATCH, PAST_LEN, A_HEADS, 2 * A_QK_DIM), 1.0),
        'cache_a_v': nrm(ks[3], (DEPTH, DEC_BATCH, PAST_LEN, A_HEADS, A_V_DIM), 1.0),
        'cache_b_k': nrm(ks[4], (DEPTH, DEC_BATCH, PAST_LEN, B_HEADS, B_DIM), 1.0),
        'cache_b_v': nrm(ks[5], (DEPTH, DEC_BATCH, PAST_LEN, B_HEADS, B_DIM), 1.0),
        'c_prompt': nrm(ks[6], (BATCH, d), 1.0),
        'c_sample': nrm(ks[7], (DEC_BATCH, d), 1.0),
        'w_ada': nrm(ks[8], (DEPTH, d, 6 * d), 0.5 * d ** -0.5),
        'b_ada': nrm(ks[9], (DEPTH, 6 * d), 0.02),
        'w_in': nrm(ks[10], (DEPTH, d, IN_COLS), d ** -0.5),
        'lam_params': nrm(ks[11], (DEPTH, 4, A_QK_DIM), 0.1),
        'g_sub_a': 1.0 + nrm(ks[12], (DEPTH, A_V_DIM), 0.02),
        'g_sub_b': 1.0 + nrm(ks[13], (DEPTH, B_DIM), 0.02),
        'w_out': nrm(ks[14], (DEPTH, MIX_WIDTH, d), MIX_WIDTH ** -0.5),
        'w_router': nrm(ks[15], (DEPTH, d, N_EXPERTS), d ** -0.5),
        'b_router': nrm(ks[16], (DEPTH, N_EXPERTS), 0.01),
        'w_gate': nrm(ks[17], (DEPTH, N_EXPERTS, d, D_EXPERT), d ** -0.5),
        'w_up': nrm(ks[18], (DEPTH, N_EXPERTS, d, D_EXPERT), d ** -0.5),
        'w_down': nrm(ks[19], (DEPTH, N_EXPERTS, D_EXPERT, d), D_EXPERT ** -0.5),
        'w_shared_gate': nrm(ks[20], (DEPTH, d, D_SHARED), d ** -0.5),
        'w_shared_up': nrm(ks[21], (DEPTH, d, D_SHARED), d ** -0.5),
        'w_shared_down': nrm(ks[22], (DEPTH, D_SHARED, d), D_SHARED ** -0.5),
        'g_final': 1.0 + nrm(ks[23], (d,), 0.02),
    }


def reference(x_prompt, x_sample, cache_a_k, cache_a_v, cache_b_k, cache_b_v, c_prompt, c_sample,
              w_ada, b_ada, w_in, lam_params, g_sub_a, g_sub_b, w_out, w_router, b_router,
              w_gate, w_up, w_down, w_shared_gate, w_shared_up, w_shared_down, g_final):
    xp, xs = x_prompt, x_sample
    rows_p, rows_s = [], []
    for l in range(DEPTH):
        p = {'w_ada': w_ada[l], 'b_ada': b_ada[l], 'w_in': w_in[l], 'lam': lam_params[l],
             'g_sub_a': g_sub_a[l], 'g_sub_b': g_sub_b[l], 'w_out': w_out[l],
             'w_router': w_router[l], 'b_router': b_router[l], 'w_gate': w_gate[l],
             'w_up': w_up[l], 'w_down': w_down[l], 'w_sg': w_shared_gate[l],
             'w_su': w_shared_up[l], 'w_sd': w_shared_down[l]}
        xp, rp = trunk_layer(xp, c_prompt, None, l, p)
        xs, rs = trunk_layer(xs, c_sample, (cache_a_k[l], cache_a_v[l], cache_b_k[l], cache_b_v[l]), l, p)
        rows_p.append(rp)
        rows_s.append(rs)
    y_prompt = rms(xp) * g_final
    y_sample = rms(xs) * g_final
    new_a_k_p = jnp.stack([r[0] for r in rows_p])
    new_a_v_p = jnp.stack([r[1] for r in rows_p])
    new_b_k_p = jnp.stack([r[2] for r in rows_p])
    new_b_v_p = jnp.stack([r[3] for r in rows_p])
    new_a_k_s = jnp.stack([r[0] for r in rows_s])
    new_a_v_s = jnp.stack([r[1] for r in rows_s])
    new_b_k_s = jnp.stack([r[2] for r in rows_s])
    new_b_v_s = jnp.stack([r[3] for r in rows_s])
    return (y_prompt, y_sample, new_a_k_p, new_a_v_p, new_b_k_p, new_b_v_p, new_a_k_s, new_a_v_s, new_b_k_s, new_b_v_s)
```

```python
import functools
import math

import jax
import jax.numpy as jnp
from jax import lax
from jax.experimental import pallas as pl
from jax.experimental.pallas import tpu as pltpu

F32 = jnp.float32
BF16 = jnp.bfloat16

D_MODEL = 1024
CHUNK = 64
A_QK_DIM = 64
A_HEADS = 4
A_WIDTH = 512
B_DIM = 64
B_HEADS = 8
B_WIDTH = 512
IN_COLS = 3 * A_WIDTH + 3 * B_WIDTH
N_EXPERTS = 64
TOP_K = 6
N_GROUPS = 8
TOPK_GROUPS = 4
D_EXPERT = 256
ROUTED_SCALE = 2.5
EPS = 1e-6
NEG_INF = -1e30
LANES = 128
QK_SCALE = A_QK_DIM ** -0.5
LAM_INIT = 0.8 - 0.6 * math.exp(-0.3 * 0)
SLOPES = tuple(2.0 ** (-8.0 * (h + 1.0) / A_HEADS) for h in range(A_HEADS))
KV_KA, KV_VA, KV_KB, KV_VB = 0, 512, 1024, 1536
VMEM_LIMIT = 56 * 1024 * 1024

_NT = (((1,), (1,)), ((), ()))


def _dot(a, b):
    return jnp.dot(a, b, preferred_element_type=F32)


def _dot_nt(a, b):
    return lax.dot_general(a, b, _NT, preferred_element_type=F32)


def _rms(x):
    return x * lax.rsqrt(jnp.mean(x * x, axis=-1, keepdims=True) + EPS)


def _silu(x):
    return x * jax.nn.sigmoid(x)


def _params(*sem):
    return pltpu.CompilerParams(dimension_semantics=sem, vmem_limit_bytes=VMEM_LIMIT)


def _ada_kernel(c_ref, w_ref, b_ref, o_ref):
    s = _silu(c_ref[...]).astype(BF16)
    o_ref[...] = _dot(s, w_ref[...].astype(BF16)) + b_ref[...]


def _ada(c, w, b):
    n, d = c.shape
    cols = w.shape[1]
    tn = 1536
    return pl.pallas_call(
        _ada_kernel,
        out_shape=jax.ShapeDtypeStruct((n, cols), F32),
        grid=(cols // tn,),
        in_specs=[pl.BlockSpec((n, d), lambda j: (0, 0)),
                  pl.BlockSpec((d, tn), lambda j: (0, j)),
                  pl.BlockSpec((1, tn), lambda j: (0, j))],
        out_specs=pl.BlockSpec((n, tn), lambda j: (0, j)),
        compiler_params=_params("arbitrary"),
        name="ada",
    )(c, w, b.reshape(1, cols))


def _inproj_kernel(x_ref, sc_ref, sh_ref, w_ref, q_ref, ka_ref, va_ref, kb_ref, vb_ref, kv_ref):
    bb, tt, d = x_ref.shape
    h = _rms(x_ref[...]) * (1.0 + sc_ref[...]) + sh_ref[...]
    proj = _dot(h.reshape(bb * tt, d).astype(BF16), w_ref[...])

    def part(lo, width):
        return proj[:, lo:lo + width].reshape(bb, tt, width)

    qa, ka, va = part(0, 512), part(512, 512), part(1024, 512)
    qb, kb, vb = part(1536, 512), part(2048, 512), part(2560, 512)
    q_ref[:, :, 0:512] = (qa * QK_SCALE).astype(BF16)
    q_ref[:, :, 512:1024] = (qb * QK_SCALE).astype(BF16)
    ka_ref[...] = ka
    va_ref[...] = va
    kb_ref[...] = kb
    vb_ref[...] = vb
    kv_ref[:, :, KV_KA:KV_KA + 512] = ka.astype(BF16)
    kv_ref[:, :, KV_VA:KV_VA + 512] = va.astype(BF16)
    kv_ref[:, :, KV_KB:KV_KB + 512] = kb.astype(BF16)
    kv_ref[:, :, KV_VB:KV_VB + 512] = vb.astype(BF16)


def _inproj(x, sc, sh, w_bf16, bb, tt):
    b, t, d = x.shape
    xspec = pl.BlockSpec((bb, tt, d), lambda i, j: (i, j, 0))
    mspec = pl.BlockSpec((bb, 1, d), lambda i, j: (i, 0, 0))

    def ospec(width):
        return pl.BlockSpec((bb, tt, width), lambda i, j: (i, j, 0))

    f32rows = jax.ShapeDtypeStruct((b, t, 512), F32)
    return pl.pallas_call(
        _inproj_kernel,
        out_shape=(jax.ShapeDtypeStruct((b, t, 1024), BF16), f32rows, f32rows, f32rows, f32rows,
                   jax.ShapeDtypeStruct((b, t, 2048), BF16)),
        grid=(b // bb, t // tt),
        in_specs=[xspec, mspec, mspec, pl.BlockSpec((d, IN_COLS), lambda i, j: (0, 0))],
        out_specs=(ospec(1024), ospec(512), ospec(512), ospec(512), ospec(512), ospec(2048)),
        compiler_params=_params("arbitrary", "arbitrary"),
        name="inproj",
    )(x, sc, sh, w_bf16)


def _lane_lo(shape):
    return lax.broadcasted_iota(jnp.int32, shape, len(shape) - 1) < 64


def _split_halves(x):
    lo = _lane_lo(x.shape)
    zero = jnp.zeros_like(x)
    return jnp.concatenate([jnp.where(lo, x, zero), jnp.where(lo, zero, x)], axis=0)


def _lam_of(lam_ref):
    lp = lam_ref[...]
    s1 = jnp.sum(lp[0:1] * lp[1:2], axis=1, keepdims=True)
    s2 = jnp.sum(lp[2:3] * lp[3:4], axis=1, keepdims=True)
    return jnp.exp(s1) - jnp.exp(s2) + LAM_INIT


def _a_update(state, s, v16):
    m, l, acc = state
    m_new = jnp.maximum(m, jnp.max(s, axis=1, keepdims=True))
    alpha = jnp.exp(m - m_new)
    p = jnp.exp(s - m_new)
    l = alpha * l + jnp.sum(p, axis=1, keepdims=True)
    acc = alpha * acc + _dot(p.astype(BF16), v16)
    return m_new, l, acc


def _a_init(rows):
    return (jnp.full((rows, 1), NEG_INF, F32), jnp.zeros((rows, 1), F32), jnp.zeros((rows, LANES), F32))


def _a_finish(state, lam, gain, nq):
    _, l, acc = state
    o = acc[:nq] / l[:nq] - lam * (acc[nq:] / l[nq:])
    return _rms(o) * gain * (1.0 - LAM_INIT)


def _diag_tiles(nq, nk):
    r = lax.broadcasted_iota(jnp.int32, (2 * nq, nk), 0)
    r = jnp.where(r >= nq, r - nq, r)
    c = lax.broadcasted_iota(jnp.int32, (2 * nq, nk), 1)
    visible = (c // CHUNK) <= (r // CHUNK)
    before = c < r
    dist_shift = (r - jnp.abs(r - c)).astype(F32)
    return visible, before, dist_shift


def _strict_upper(n):
    j = lax.broadcasted_iota(jnp.int32, (n, n), 0)
    s = lax.broadcasted_iota(jnp.int32, (n, n), 1)
    return jnp.where(j > s, 1.0, 0.0).astype(BF16)


def _b_block(z, v16, tri, carry, acc, before, nq):
    sp = jnp.maximum(z, 0.0) + jnp.log(1.0 + jnp.exp(-jnp.abs(z)))
    logkeep = -sp
    if before is not None:
        logkeep = jnp.where(before, logkeep, 0.0)
    hi = logkeep.astype(BF16)
    lo = (logkeep - hi.astype(F32)).astype(BF16)
    after = carry + _dot(hi, tri) + _dot(lo, tri)
    a = jnp.exp((z - sp) + after)
    if before is not None:
        a = jnp.where(before, a, 0.0)
    a16 = a.astype(BF16)
    vlo = _lane_lo(v16.shape)
    vzero = jnp.zeros_like(v16)
    acc = acc + _dot(a16[:nq], jnp.where(vlo, v16, vzero)) + _dot(a16[nq:], jnp.where(vlo, vzero, v16))
    carry = carry + jnp.sum(logkeep, axis=1, keepdims=True)
    return carry, acc


def _b_finish(acc, gain2):
    lo = _lane_lo(acc.shape)
    ss = acc * acc
    s_lo = jnp.sum(jnp.where(lo, ss, 0.0), axis=1, keepdims=True)
    s_hi = jnp.sum(jnp.where(lo, 0.0, ss), axis=1, keepdims=True)
    ms = jnp.where(lo, s_lo, s_hi) * (1.0 / B_DIM)
    return acc * lax.rsqrt(ms + EPS) * gain2


def _attn_prompt_kernel(lam_ref, ga_ref, gb_ref, q_ref, kv_ref, o_ref):
    bq = q_ref.shape[1]
    i = pl.program_id(1)
    q0 = i * bq
    lam = _lam_of(lam_ref)
    visible, before, dist_shift = _diag_tiles(bq, bq)
    col = lax.broadcasted_iota(jnp.int32, (1, bq), 1)
    tri = _strict_upper(bq)

    for h in range(A_HEADS):
        cs = slice(h * LANES, (h + 1) * LANES)
        qs = _split_halves(q_ref[0, :, cs])
        slope = SLOPES[h]

        def a_body(j, state, qs=qs, slope=slope, h=h):
            k0 = pl.multiple_of(j * bq, bq)
            k16 = kv_ref[0, pl.ds(k0, bq), KV_KA + h * LANES:KV_KA + (h + 1) * LANES]
            v16 = kv_ref[0, pl.ds(k0, bq), KV_VA + h * LANES:KV_VA + (h + 1) * LANES]
            s = _dot_nt(qs, k16) + slope * (col + (k0 - q0)).astype(F32)
            return _a_update(state, s, v16)

        state = lax.fori_loop(0, i, a_body, _a_init(2 * bq))
        k16 = kv_ref[0, pl.ds(pl.multiple_of(q0, bq), bq), KV_KA + h * LANES:KV_KA + (h + 1) * LANES]
        v16 = kv_ref[0, pl.ds(pl.multiple_of(q0, bq), bq), KV_VA + h * LANES:KV_VA + (h + 1) * LANES]
        s = jnp.where(visible, _dot_nt(qs, k16) + slope * dist_shift, NEG_INF)
        state = _a_update(state, s, v16)
        o_ref[0, :, cs] = _a_finish(state, lam, ga_ref[...], bq).astype(o_ref.dtype)

    for p in range(B_HEADS // 2):
        qs = _split_halves(q_ref[0, :, A_WIDTH + p * LANES:A_WIDTH + (p + 1) * LANES])
        kcs = slice(KV_KB + p * LANES, KV_KB + (p + 1) * LANES)
        vcs = slice(KV_VB + p * LANES, KV_VB + (p + 1) * LANES)
        kd = pl.ds(pl.multiple_of(q0, bq), bq)
        carry, acc = _b_block(_dot_nt(qs, kv_ref[0, kd, kcs]), kv_ref[0, kd, vcs], tri,
                              jnp.zeros((2 * bq, 1), F32), jnp.zeros((bq, LANES), F32), before, bq)

        def b_body(jj, st, qs=qs, kcs=kcs, vcs=vcs):
            kd = pl.ds(pl.multiple_of((i - 1 - jj) * bq, bq), bq)
            return _b_block(_dot_nt(qs, kv_ref[0, kd, kcs]), kv_ref[0, kd, vcs], tri, st[0], st[1], None, bq)

        carry, acc = lax.fori_loop(0, i, b_body, (carry, acc))
        o_ref[0, :, A_WIDTH + p * LANES:A_WIDTH + (p + 1) * LANES] = _b_finish(acc, gb_ref[...]).astype(o_ref.dtype)


def _attn_prompt(lam_p, ga, gb2, q, kv, bq):
    b, t, _ = q.shape
    small = lambda shape: pl.BlockSpec(shape, lambda bi, i: (0, 0))
    return pl.pallas_call(
        _attn_prompt_kernel,
        out_shape=jax.ShapeDtypeStruct((b, t, D_MODEL), BF16),
        grid=(b, t // bq),
        in_specs=[small((4, A_QK_DIM)), small((1, LANES)), small((1, LANES)),
                  pl.BlockSpec((1, bq, D_MODEL), lambda bi, i: (bi, i, 0)),
                  pl.BlockSpec((1, t, 2048), lambda bi, i: (bi, 0, 0))],
        out_specs=pl.BlockSpec((1, bq, D_MODEL), lambda bi, i: (bi, i, 0)),
        compiler_params=_params("arbitrary", "arbitrary"),
        name="attn_prompt",
    )(lam_p, ga, gb2, q, kv)


def _attn_sample_kernel(lam_ref, ga_ref, gb_ref, q_ref, kv_ref, cak_ref, cav_ref, cbk_ref, cbv_ref, o_ref):
    nq = q_ref.shape[1]
    past = cak_ref.shape[1]
    pb = 256
    lam = _lam_of(lam_ref)
    visible, before, dist_shift = _diag_tiles(nq, nq)
    pcol = lax.broadcasted_iota(jnp.int32, (1, past), 1)
    tri_new = _strict_upper(nq)
    tri_past = _strict_upper(pb)

    for h in range(A_HEADS):
        cs = slice(h * LANES, (h + 1) * LANES)
        qs = _split_halves(q_ref[0, :, cs])
        slope = SLOPES[h]
        s_past = _dot_nt(qs, cak_ref[0, :, cs].astype(BF16)) + slope * (pcol - past).astype(F32)
        state = _a_update(_a_init(2 * nq), s_past, cav_ref[0, :, cs].astype(BF16))
        k16 = kv_ref[0, :, KV_KA + h * LANES:KV_KA + (h + 1) * LANES]
        v16 = kv_ref[0, :, KV_VA + h * LANES:KV_VA + (h + 1) * LANES]
        s_new = jnp.where(visible, _dot_nt(qs, k16) + slope * dist_shift, NEG_INF)
        state = _a_update(state, s_new, v16)
        o_ref[0, :, cs] = _a_finish(state, lam, ga_ref[...], nq).astype(o_ref.dtype)

    for p in range(B_HEADS // 2):
        cs = slice(p * LANES, (p + 1) * LANES)
        qs = _split_halves(q_ref[0, :, A_WIDTH + p * LANES:A_WIDTH + (p + 1) * LANES])
        k16 = kv_ref[0, :, KV_KB + p * LANES:KV_KB + (p + 1) * LANES]
        v16 = kv_ref[0, :, KV_VB + p * LANES:KV_VB + (p + 1) * LANES]
        carry, acc = _b_block(_dot_nt(qs, k16), v16, tri_new,
                              jnp.zeros((2 * nq, 1), F32), jnp.zeros((nq, LANES), F32), before, nq)
        for j in reversed(range(past // pb)):
            rows = slice(j * pb, (j + 1) * pb)
            z = _dot_nt(qs, cbk_ref[0, rows, cs].astype(BF16))
            carry, acc = _b_block(z, cbv_ref[0, rows, cs].astype(BF16), tri_past, carry, acc, None, nq)
        o_ref[0, :, A_WIDTH + p * LANES:A_WIDTH + (p + 1) * LANES] = _b_finish(acc, gb_ref[...]).astype(o_ref.dtype)


def _attn_sample(lam_p, ga, gb2, q, kv, cak, cav, cbk, cbv):
    b, t, _ = q.shape
    past = cak.shape[1]
    small = lambda shape: pl.BlockSpec(shape, lambda bi: (0, 0))
    per_b = lambda rows, width: pl.BlockSpec((1, rows, width), lambda bi: (bi, 0, 0))
    return pl.pallas_call(
        _attn_sample_kernel,
        out_shape=jax.ShapeDtypeStruct((b, t, D_MODEL), BF16),
        grid=(b,),
        in_specs=[small((4, A_QK_DIM)), small((1, LANES)), small((1, LANES)),
                  per_b(t, D_MODEL), per_b(t, 2048),
                  per_b(past, 512), per_b(past, 512), per_b(past, 512), per_b(past, 512)],
        out_specs=per_b(t, D_MODEL),
        compiler_params=_params("arbitrary"),
        name="attn_sample",
    )(lam_p, ga, gb2, q, kv, cak, cav, cbk, cbv)


def _route(logits_t, bias_col):
    n = logits_t.shape[1]
    scores = jax.nn.sigmoid(logits_t)
    sel3 = (scores + bias_col).reshape(N_GROUPS, 8, n)
    eio = lax.broadcasted_iota(jnp.int32, (N_GROUPS, 8, n), 1)
    m1 = jnp.max(sel3, axis=1, keepdims=True)
    i1 = jnp.min(jnp.where(sel3 == m1, eio, 8), axis=1, keepdims=True)
    m2 = jnp.max(jnp.where(eio == i1, -jnp.inf, sel3), axis=1, keepdims=True)
    gscore = (m1 + m2).reshape(N_GROUPS, n)
    gio = lax.broadcasted_iota(jnp.int32, (N_GROUPS, n), 0)
    keep = jnp.zeros((N_GROUPS, n), F32)
    for _ in range(TOPK_GROUPS):
        g = jnp.max(gscore, axis=0, keepdims=True)
        gi = jnp.min(jnp.where(gscore == g, gio, N_GROUPS), axis=0, keepdims=True)
        hit = gio == gi
        keep = jnp.where(hit, 1.0, keep)
        gscore = jnp.where(hit, -jnp.inf, gscore)
    cur = jnp.where(keep.reshape(N_GROUPS, 1, n) > 0.0, sel3, NEG_INF).reshape(N_EXPERTS, n)
    xio = lax.broadcasted_iota(jnp.int32, (N_EXPERTS, n), 0)
    chosen = jnp.zeros((N_EXPERTS, n), F32)
    for _ in range(TOP_K):
        mx = jnp.max(cur, axis=0, keepdims=True)
        ei = jnp.min(jnp.where(cur == mx, xio, N_EXPERTS), axis=0, keepdims=True)
        hit = xio == ei
        chosen = jnp.where(hit, 1.0, chosen)
        cur = jnp.where(hit, -jnp.inf, cur)
    wsel = jnp.where(chosen > 0.0, scores, 0.0)
    return wsel / jnp.sum(wsel, axis=0, keepdims=True) * ROUTED_SCALE


def _outproj_kernel(o_ref, x_ref, g1_ref, sc_ref, sh_ref, wout_ref, wrt_ref, br_ref, x1_ref, h2_ref, comb_ref):
    bb, tt, d = x_ref.shape
    n = bb * tt
    proj = _dot(o_ref[...].reshape(n, d), wout_ref[...]).reshape(bb, tt, d)
    x1 = x_ref[...] + g1_ref[...] * proj
    x1_ref[...] = x1
    h2 = (_rms(x1) * (1.0 + sc_ref[...]) + sh_ref[...]).reshape(n, d)
    h2hi = h2.astype(BF16)
    h2_ref[...] = h2hi.reshape(bb, tt, d)
    h2lo = (h2 - h2hi.astype(F32)).astype(BF16)
    w = wrt_ref[...]
    whi = w.astype(BF16)
    wlo = (w - whi.astype(F32)).astype(BF16)
    logits_t = _dot_nt(whi, h2hi) + _dot_nt(whi, h2lo) + _dot_nt(wlo, h2hi)
    comb_t = _route(logits_t, br_ref[...])
    padded = jnp.concatenate([comb_t, jnp.zeros_like(comb_t)], axis=0)
    comb_ref[...] = padded.T.reshape(bb, tt, LANES)


def _outproj(o, x, g1, sc, sh, wout_bf16, wr_t, br_col, bb, tt):
    b, t, d = x.shape
    tok = pl.BlockSpec((bb, tt, d), lambda i, j: (i, j, 0))
    mspec = pl.BlockSpec((bb, 1, d), lambda i, j: (i, 0, 0))
    full = lambda shape: pl.BlockSpec(shape, lambda i, j: (0, 0))
    return pl.pallas_call(
        _outproj_kernel,
        out_shape=(jax.ShapeDtypeStruct((b, t, d), F32), jax.ShapeDtypeStruct((b, t, d), BF16),
                   jax.ShapeDtypeStruct((b, t, LANES), F32)),
        grid=(b // bb, t // tt),
        in_specs=[tok, tok, mspec, mspec, mspec, full((d, d)), full((N_EXPERTS, d)), full((N_EXPERTS, 1))],
        out_specs=(tok, tok, pl.BlockSpec((bb, tt, LANES), lambda i, j: (i, j, 0))),
        compiler_params=_params("arbitrary", "arbitrary"),
        name="outproj_router",
    )(o, x, g1, sc, sh, wout_bf16, wr_t, br_col)


def _moe_kernel(h_ref, comb_ref, wg_ref, wu_ref, wd_ref, out_ref):
    e = pl.program_id(1)

    @pl.when(e == 0)
    def _():
        out_ref[...] = jnp.zeros_like(out_ref)

    h = h_ref[...]
    g = _dot(h, wg_ref[0])
    u = _dot(h, wu_ref[0])
    lane = lax.broadcasted_iota(jnp.int32, comb_ref.shape, 1)
    ce = jnp.sum(jnp.where(lane == e, comb_ref[...], 0.0), axis=1, keepdims=True)
    a = _silu(g) * u * ce
    out_ref[...] += _dot(a.astype(BF16), wd_ref[0])


def _moe_dense(h2, comb, wg, wu, wd, tm):
    n, d = h2.shape
    return pl.pallas_call(
        _moe_kernel,
        out_shape=jax.ShapeDtypeStruct((n, d), F32),
        grid=(n // tm, N_EXPERTS),
        in_specs=[pl.BlockSpec((tm, d), lambda i, e: (i, 0)),
                  pl.BlockSpec((tm, LANES), lambda i, e: (i, 0)),
                  pl.BlockSpec((1, d, D_EXPERT), lambda i, e: (e, 0, 0)),
                  pl.BlockSpec((1, d, D_EXPERT), lambda i, e: (e, 0, 0)),
                  pl.BlockSpec((1, D_EXPERT, d), lambda i, e: (e, 0, 0))],
        out_specs=pl.BlockSpec((tm, d), lambda i, e: (i, 0)),
        compiler_params=_params("arbitrary", "arbitrary"),
        name="moe_dense",
    )(h2, comb, wg, wu, wd)


def _final_kernel(x1_ref, h2_ref, r_ref, g2_ref, wsg_ref, wsu_ref, wsd_ref, gf_ref, y_ref):
    bb, tt, d = x1_ref.shape
    h = h2_ref[...].reshape(bb * tt, d)
    act = _silu(_dot(h, wsg_ref[...])) * _dot(h, wsu_ref[...])
    shared = _dot(act.astype(BF16), wsd_ref[...]).reshape(bb, tt, d)
    x2 = x1_ref[...] + g2_ref[...] * (r_ref[...] + shared)
    y_ref[...] = _rms(x2) * gf_ref[...]


def _final(x1, h2, routed, g2, wsg, wsu, wsd, gf, bb, tt):
    b, t, d = x1.shape
    tok = pl.BlockSpec((bb, tt, d), lambda i, j: (i, j, 0))
    full = lambda shape: pl.BlockSpec(shape, lambda i, j: (0,) * len(shape))
    return pl.pallas_call(
        _final_kernel,
        out_shape=jax.ShapeDtypeStruct((b, t, d), F32),
        grid=(b // bb, t // tt),
        in_specs=[tok, tok, tok, pl.BlockSpec((bb, 1, d), lambda i, j: (i, 0, 0)),
                  full(wsg.shape), full(wsu.shape), full(wsd.shape), full((1, 1, d))],
        out_specs=tok,
        compiler_params=_params("arbitrary", "arbitrary"),
        name="final",
    )(x1, h2, routed, g2, wsg, wsu, wsd, gf)


def kernel(x_prompt, x_sample, cache_a_k, cache_a_v, cache_b_k, cache_b_v, c_prompt, c_sample, w_ada, b_ada, w_in, lam_params, g_sub_a, g_sub_b, w_out, w_router, b_router, w_gate, w_up, w_down, w_shared_gate, w_shared_up, w_shared_down, g_final):
    bp, tp, d = x_prompt.shape
    bs, ts, _ = x_sample.shape
    past = cache_a_k.shape[2]

    mod = _ada(jnp.concatenate([c_prompt, c_sample], axis=0), w_ada[0], b_ada[0])
    mod = mod.reshape(bp + bs, 6, 1, d)
    sh1, sc1, g1, sh2, sc2, g2 = (mod[:, k] for k in range(6))

    w_in16 = w_in[0].astype(BF16)
    w_out16 = w_out[0].astype(BF16)
    wr_t = w_router[0].T
    br_col = b_router[0].reshape(N_EXPERTS, 1)
    wg16, wu16, wd16 = w_gate[0].astype(BF16), w_up[0].astype(BF16), w_down[0].astype(BF16)
    wsg16, wsu16, wsd16 = (w_shared_gate[0].astype(BF16), w_shared_up[0].astype(BF16),
                           w_shared_down[0].astype(BF16))
    ga = g_sub_a[0].reshape(1, LANES)
    gb2 = jnp.concatenate([g_sub_b[0], g_sub_b[0]]).reshape(1, LANES)
    gf = g_final.reshape(1, 1, d)
    lam_p = lam_params[0]

    def stream(x, sl, bb, tt, attn):
        q, ka, va, kb, vb, kv = _inproj(x, sc1[sl], sh1[sl], w_in16, bb, tt)
        o = attn(q, kv)
        x1, h2, comb = _outproj(o, x, g1[sl], sc2[sl], sh2[sl], w_out16, wr_t, br_col, bb, tt)
        b, t, _ = x.shape
        routed = _moe_dense(h2.reshape(b * t, d), comb.reshape(b * t, LANES), wg16, wu16, wd16, 1024)
        y = _final(x1, h2, routed.reshape(b, t, d), g2[sl], wsg16, wsu16, wsd16, gf, bb, tt)
        return y, (ka, va, kb, vb)

    y_p, rows_p = stream(x_prompt, slice(0, bp), 1, 512,
                         lambda q, kv: _attn_prompt(lam_p, ga, gb2, q, kv, 256))
    caches = (cache_a_k[0].reshape(bs, past, 512), cache_a_v[0].reshape(bs, past, 512),
              cache_b_k[0].reshape(bs, past, 512), cache_b_v[0].reshape(bs, past, 512))
    y_s, rows_s = stream(x_sample, slice(bp, bp + bs), 8, ts,
                         lambda q, kv: _attn_sample(lam_p, ga, gb2, q, kv, *caches))

    def shape_rows(rows, b, t):
        ka, va, kb, vb = rows
        return (ka.reshape(1, b, t, A_HEADS, 2 * A_QK_DIM), va.reshape(1, b, t, A_HEADS, 2 * A_QK_DIM),
                kb.reshape(1, b, t, B_HEADS, B_DIM), vb.reshape(1, b, t, B_HEADS, B_DIM))

    return (y_p, y_s) + shape_rows(rows_p, bp, tp) + shape_rows(rows_s, bs, ts)
```

```python
import functools
import math

import jax
import jax.numpy as jnp
from jax import lax
from jax.experimental import pallas as pl
from jax.experimental.pallas import tpu as pltpu

F32 = jnp.float32
BF16 = jnp.bfloat16

D_MODEL = 1024
CHUNK = 64
A_QK_DIM = 64
A_HEADS = 4
A_WIDTH = 512
B_DIM = 64
B_HEADS = 8
B_WIDTH = 512
IN_COLS = 3 * A_WIDTH + 3 * B_WIDTH
N_EXPERTS = 64
TOP_K = 6
N_GROUPS = 8
TOPK_GROUPS = 4
D_EXPERT = 256
ROUTED_SCALE = 2.5
EPS = 1e-6
NEG_INF = -1e30
LANES = 128
QK_SCALE = A_QK_DIM ** -0.5
LAM_INIT = 0.8 - 0.6 * math.exp(-0.3 * 0)
SLOPES = tuple(2.0 ** (-8.0 * (h + 1.0) / A_HEADS) for h in range(A_HEADS))
KV_KA, KV_VA, KV_KB, KV_VB = 0, 512, 1024, 1536
VMEM_LIMIT = 56 * 1024 * 1024
TM = 256
SEG = 16
SEG_BITS = 5
CH = 256
RMAX = 2560
TR = 512
FILL_BITS = 5

_NT = (((1,), (1,)), ((), ()))


def _dot(a, b):
    return jnp.dot(a, b, preferred_element_type=F32)


def _dot_nt(a, b):
    return lax.dot_general(a, b, _NT, preferred_element_type=F32)


def _rms(x):
    return x * lax.rsqrt(jnp.mean(x * x, axis=-1, keepdims=True) + EPS)


def _silu(x):
    return x * jax.nn.sigmoid(x)


def _params(*sem):
    return pltpu.CompilerParams(dimension_semantics=sem, vmem_limit_bytes=VMEM_LIMIT)


def _ada_kernel(c_ref, w_ref, b_ref, o_ref):
    s = _silu(c_ref[...]).astype(BF16)
    o_ref[...] = _dot(s, w_ref[...].astype(BF16)) + b_ref[...]


def _ada(c, w, b):
    n, d = c.shape
    cols = w.shape[1]
    tn = 1536
    return pl.pallas_call(
        _ada_kernel,
        out_shape=jax.ShapeDtypeStruct((n, cols), F32),
        grid=(cols // tn,),
        in_specs=[pl.BlockSpec((n, d), lambda j: (0, 0)),
                  pl.BlockSpec((d, tn), lambda j: (0, j)),
                  pl.BlockSpec((1, tn), lambda j: (0, j))],
        out_specs=pl.BlockSpec((n, tn), lambda j: (0, j)),
        compiler_params=_params("arbitrary"),
        name="ada",
    )(c, w, b.reshape(1, cols))


def _inproj_kernel(x_ref, sc_ref, sh_ref, w_ref, q_ref, ka_ref, va_ref, kb_ref, vb_ref, kv_ref):
    bb, tt, d = x_ref.shape
    h = _rms(x_ref[...]) * (1.0 + sc_ref[...]) + sh_ref[...]
    proj = _dot(h.reshape(bb * tt, d).astype(BF16), w_ref[...])

    def part(lo, width):
        return proj[:, lo:lo + width].reshape(bb, tt, width)

    qa, ka, va = part(0, 512), part(512, 512), part(1024, 512)
    qb, kb, vb = part(1536, 512), part(2048, 512), part(2560, 512)
    q_ref[:, :, 0:512] = (qa * QK_SCALE).astype(BF16)
    q_ref[:, :, 512:1024] = (qb * QK_SCALE).astype(BF16)
    ka_ref[...] = ka
    va_ref[...] = va
    kb_ref[...] = kb
    vb_ref[...] = vb
    kv_ref[:, :, KV_KA:KV_KA + 512] = ka.astype(BF16)
    kv_ref[:, :, KV_VA:KV_VA + 512] = va.astype(BF16)
    kv_ref[:, :, KV_KB:KV_KB + 512] = kb.astype(BF16)
    kv_ref[:, :, KV_VB:KV_VB + 512] = vb.astype(BF16)


def _inproj(x, sc, sh, w_bf16, bb, tt):
    b, t, d = x.shape
    xspec = pl.BlockSpec((bb, tt, d), lambda i, j: (i, j, 0))
    mspec = pl.BlockSpec((bb, 1, d), lambda i, j: (i, 0, 0))

    def ospec(width):
        return pl.BlockSpec((bb, tt, width), lambda i, j: (i, j, 0))

    f32rows = jax.ShapeDtypeStruct((b, t, 512), F32)
    return pl.pallas_call(
        _inproj_kernel,
        out_shape=(jax.ShapeDtypeStruct((b, t, 1024), BF16), f32rows, f32rows, f32rows, f32rows,
                   jax.ShapeDtypeStruct((b, t, 2048), BF16)),
        grid=(b // bb, t // tt),
        in_specs=[xspec, mspec, mspec, pl.BlockSpec((d, IN_COLS), lambda i, j: (0, 0))],
        out_specs=(ospec(1024), ospec(512), ospec(512), ospec(512), ospec(512), ospec(2048)),
        compiler_params=_params("arbitrary", "arbitrary"),
        name="inproj",
    )(x, sc, sh, w_bf16)


def _lane_lo(shape):
    return lax.broadcasted_iota(jnp.int32, shape, len(shape) - 1) < 64


def _split_halves(x):
    lo = _lane_lo(x.shape)
    zero = jnp.zeros_like(x)
    return jnp.concatenate([jnp.where(lo, x, zero), jnp.where(lo, zero, x)], axis=0)


def _lam_of(lam_ref):
    lp = lam_ref[...]
    s1 = jnp.sum(lp[0:1] * lp[1:2], axis=1, keepdims=True)
    s2 = jnp.sum(lp[2:3] * lp[3:4], axis=1, keepdims=True)
    return jnp.exp(s1) - jnp.exp(s2) + LAM_INIT


def _a_update(state, s, v16):
    m, l, acc = state
    m_new = jnp.maximum(m, jnp.max(s, axis=1, keepdims=True))
    alpha = jnp.exp(m - m_new)
    p = jnp.exp(s - m_new)
    l = alpha * l + jnp.sum(p, axis=1, keepdims=True)
    acc = alpha * acc + _dot(p.astype(BF16), v16)
    return m_new, l, acc


def _a_init(rows):
    return (jnp.full((rows, 1), NEG_INF, F32), jnp.zeros((rows, 1), F32), jnp.zeros((rows, LANES), F32))


def _a_finish(state, lam, gain, nq):
    _, l, acc = state
    o = acc[:nq] / l[:nq] - lam * (acc[nq:] / l[nq:])
    return _rms(o) * gain * (1.0 - LAM_INIT)


def _diag_tiles(nq, nk):
    r = lax.broadcasted_iota(jnp.int32, (2 * nq, nk), 0)
    r = jnp.where(r >= nq, r - nq, r)
    c = lax.broadcasted_iota(jnp.int32, (2 * nq, nk), 1)
    visible = (c // CHUNK) <= (r // CHUNK)
    before = c < r
    dist_shift = (r - jnp.abs(r - c)).astype(F32)
    return visible, before, dist_shift


def _strict_upper(n):
    j = lax.broadcasted_iota(jnp.int32, (n, n), 0)
    s = lax.broadcasted_iota(jnp.int32, (n, n), 1)
    return jnp.where(j > s, 1.0, 0.0).astype(BF16)


def _b_block(z, v16, tri, carry, acc, before, nq):
    sp = jnp.maximum(z, 0.0) + jnp.log(1.0 + jnp.exp(-jnp.abs(z)))
    logkeep = -sp
    if before is not None:
        logkeep = jnp.where(before, logkeep, 0.0)
    hi = logkeep.astype(BF16)
    lo = (logkeep - hi.astype(F32)).astype(BF16)
    after = carry + _dot(hi, tri) + _dot(lo, tri)
    a = jnp.exp((z - sp) + after)
    if before is not None:
        a = jnp.where(before, a, 0.0)
    a16 = a.astype(BF16)
    vlo = _lane_lo(v16.shape)
    vzero = jnp.zeros_like(v16)
    acc = acc + _dot(a16[:nq], jnp.where(vlo, v16, vzero)) + _dot(a16[nq:], jnp.where(vlo, vzero, v16))
    carry = carry + jnp.sum(logkeep, axis=1, keepdims=True)
    return carry, acc


def _b_finish(acc, gain2):
    lo = _lane_lo(acc.shape)
    ss = acc * acc
    s_lo = jnp.sum(jnp.where(lo, ss, 0.0), axis=1, keepdims=True)
    s_hi = jnp.sum(jnp.where(lo, 0.0, ss), axis=1, keepdims=True)
    ms = jnp.where(lo, s_lo, s_hi) * (1.0 / B_DIM)
    return acc * lax.rsqrt(ms + EPS) * gain2


def _attn_prompt_kernel(lam_ref, ga_ref, gb_ref, q_ref, kv_ref, o_ref):
    bq = q_ref.shape[1]
    i = pl.program_id(1)
    q0 = i * bq
    lam = _lam_of(lam_ref)
    visible, before, dist_shift = _diag_tiles(bq, bq)
    col = lax.broadcasted_iota(jnp.int32, (1, bq), 1)
    tri = _strict_upper(bq)

    for h in range(A_HEADS):
        cs = slice(h * LANES, (h + 1) * LANES)
        qs = _split_halves(q_ref[0, :, cs])
        slope = SLOPES[h]

        def a_body(j, state, qs=qs, slope=slope, h=h):
            k0 = pl.multiple_of(j * bq, bq)
            k16 = kv_ref[0, pl.ds(k0, bq), KV_KA + h * LANES:KV_KA + (h + 1) * LANES]
            v16 = kv_ref[0, pl.ds(k0, bq), KV_VA + h * LANES:KV_VA + (h + 1) * LANES]
            s = _dot_nt(qs, k16) + slope * (col + (k0 - q0)).astype(F32)
            return _a_update(state, s, v16)

        state = lax.fori_loop(0, i, a_body, _a_init(2 * bq))
        k16 = kv_ref[0, pl.ds(pl.multiple_of(q0, bq), bq), KV_KA + h * LANES:KV_KA + (h + 1) * LANES]
        v16 = kv_ref[0, pl.ds(pl.multiple_of(q0, bq), bq), KV_VA + h * LANES:KV_VA + (h + 1) * LANES]
        s = jnp.where(visible, _dot_nt(qs, k16) + slope * dist_shift, NEG_INF)
        state = _a_update(state, s, v16)
        o_ref[0, :, cs] = _a_finish(state, lam, ga_ref[...], bq).astype(o_ref.dtype)

    for p in range(B_HEADS // 2):
        qs = _split_halves(q_ref[0, :, A_WIDTH + p * LANES:A_WIDTH + (p + 1) * LANES])
        kcs = slice(KV_KB + p * LANES, KV_KB + (p + 1) * LANES)
        vcs = slice(KV_VB + p * LANES, KV_VB + (p + 1) * LANES)
        kd = pl.ds(pl.multiple_of(q0, bq), bq)
        carry, acc = _b_block(_dot_nt(qs, kv_ref[0, kd, kcs]), kv_ref[0, kd, vcs], tri,
                              jnp.zeros((2 * bq, 1), F32), jnp.zeros((bq, LANES), F32), before, bq)

        def b_body(jj, st, qs=qs, kcs=kcs, vcs=vcs):
            kd = pl.ds(pl.multiple_of((i - 1 - jj) * bq, bq), bq)
            return _b_block(_dot_nt(qs, kv_ref[0, kd, kcs]), kv_ref[0, kd, vcs], tri, st[0], st[1], None, bq)

        carry, acc = lax.fori_loop(0, i, b_body, (carry, acc))
        o_ref[0, :, A_WIDTH + p * LANES:A_WIDTH + (p + 1) * LANES] = _b_finish(acc, gb_ref[...]).astype(o_ref.dtype)


def _attn_prompt(lam_p, ga, gb2, q, kv, bq):
    b, t, _ = q.shape
    small = lambda shape: pl.BlockSpec(shape, lambda bi, i: (0, 0))
    return pl.pallas_call(
        _attn_prompt_kernel,
        out_shape=jax.ShapeDtypeStruct((b, t, D_MODEL), BF16),
        grid=(b, t // bq),
        in_specs=[small((4, A_QK_DIM)), small((1, LANES)), small((1, LANES)),
                  pl.BlockSpec((1, bq, D_MODEL), lambda bi, i: (bi, i, 0)),
                  pl.BlockSpec((1, t, 2048), lambda bi, i: (bi, 0, 0))],
        out_specs=pl.BlockSpec((1, bq, D_MODEL), lambda bi, i: (bi, i, 0)),
        compiler_params=_params("arbitrary", "arbitrary"),
        name="attn_prompt",
    )(lam_p, ga, gb2, q, kv)


def _attn_sample_kernel(lam_ref, ga_ref, gb_ref, q_ref, kv_ref, cak_ref, cav_ref, cbk_ref, cbv_ref, o_ref):
    nq = q_ref.shape[1]
    past = cak_ref.shape[1]
    pb = 256
    lam = _lam_of(lam_ref)
    visible, before, dist_shift = _diag_tiles(nq, nq)
    pcol = lax.broadcasted_iota(jnp.int32, (1, past), 1)
    tri_new = _strict_upper(nq)
    tri_past = _strict_upper(pb)

    for h in range(A_HEADS):
        cs = slice(h * LANES, (h + 1) * LANES)
        qs = _split_halves(q_ref[0, :, cs])
        slope = SLOPES[h]
        s_past = _dot_nt(qs, cak_ref[0, :, cs].astype(BF16)) + slope * (pcol - past).astype(F32)
        state = _a_update(_a_init(2 * nq), s_past, cav_ref[0, :, cs].astype(BF16))
        k16 = kv_ref[0, :, KV_KA + h * LANES:KV_KA + (h + 1) * LANES]
        v16 = kv_ref[0, :, KV_VA + h * LANES:KV_VA + (h + 1) * LANES]
        s_new = jnp.where(visible, _dot_nt(qs, k16) + slope * dist_shift, NEG_INF)
        state = _a_update(state, s_new, v16)
        o_ref[0, :, cs] = _a_finish(state, lam, ga_ref[...], nq).astype(o_ref.dtype)

    for p in range(B_HEADS // 2):
        cs = slice(p * LANES, (p + 1) * LANES)
        qs = _split_halves(q_ref[0, :, A_WIDTH + p * LANES:A_WIDTH + (p + 1) * LANES])
        k16 = kv_ref[0, :, KV_KB + p * LANES:KV_KB + (p + 1) * LANES]
        v16 = kv_ref[0, :, KV_VB + p * LANES:KV_VB + (p + 1) * LANES]
        carry, acc = _b_block(_dot_nt(qs, k16), v16, tri_new,
                              jnp.zeros((2 * nq, 1), F32), jnp.zeros((nq, LANES), F32), before, nq)
        for j in reversed(range(past // pb)):
            rows = slice(j * pb, (j + 1) * pb)
            z = _dot_nt(qs, cbk_ref[0, rows, cs].astype(BF16))
            carry, acc = _b_block(z, cbv_ref[0, rows, cs].astype(BF16), tri_past, carry, acc, None, nq)
        o_ref[0, :, A_WIDTH + p * LANES:A_WIDTH + (p + 1) * LANES] = _b_finish(acc, gb_ref[...]).astype(o_ref.dtype)


def _attn_sample(lam_p, ga, gb2, q, kv, cak, cav, cbk, cbv):
    b, t, _ = q.shape
    past = cak.shape[1]
    small = lambda shape: pl.BlockSpec(shape, lambda bi: (0, 0))
    per_b = lambda rows, width: pl.BlockSpec((1, rows, width), lambda bi: (bi, 0, 0))
    return pl.pallas_call(
        _attn_sample_kernel,
        out_shape=jax.ShapeDtypeStruct((b, t, D_MODEL), BF16),
        grid=(b,),
        in_specs=[small((4, A_QK_DIM)), small((1, LANES)), small((1, LANES)),
                  per_b(t, D_MODEL), per_b(t, 2048),
                  per_b(past, 512), per_b(past, 512), per_b(past, 512), per_b(past, 512)],
        out_specs=per_b(t, D_MODEL),
        compiler_params=_params("arbitrary"),
        name="attn_sample",
    )(lam_p, ga, gb2, q, kv, cak, cav, cbk, cbv)


def _route(logits_t, bias_col):
    n = logits_t.shape[1]
    scores = jax.nn.sigmoid(logits_t)
    sel3 = (scores + bias_col).reshape(N_GROUPS, 8, n)
    eio = lax.broadcasted_iota(jnp.int32, (N_GROUPS, 8, n), 1)
    m1 = jnp.max(sel3, axis=1, keepdims=True)
    i1 = jnp.min(jnp.where(sel3 == m1, eio, 8), axis=1, keepdims=True)
    m2 = jnp.max(jnp.where(eio == i1, -jnp.inf, sel3), axis=1, keepdims=True)
    gscore = (m1 + m2).reshape(N_GROUPS, n)
    gio = lax.broadcasted_iota(jnp.int32, (N_GROUPS, n), 0)
    keep = jnp.zeros((N_GROUPS, n), F32)
    for _ in range(TOPK_GROUPS):
        g = jnp.max(gscore, axis=0, keepdims=True)
        gi = jnp.min(jnp.where(gscore == g, gio, N_GROUPS), axis=0, keepdims=True)
        hit = gio == gi
        keep = jnp.where(hit, 1.0, keep)
        gscore = jnp.where(hit, -jnp.inf, gscore)
    cur = jnp.where(keep.reshape(N_GROUPS, 1, n) > 0.0, sel3, NEG_INF).reshape(N_EXPERTS, n)
    xio = lax.broadcasted_iota(jnp.int32, (N_EXPERTS, n), 0)
    chosen = jnp.zeros((N_EXPERTS, n), F32)
    ids, raw = [], []
    for _ in range(TOP_K):
        mx = jnp.max(cur, axis=0, keepdims=True)
        ei = jnp.min(jnp.where(cur == mx, xio, N_EXPERTS), axis=0, keepdims=True)
        hit = xio == ei
        chosen = jnp.where(hit, 1.0, chosen)
        cur = jnp.where(hit, -jnp.inf, cur)
        ids.append(ei.astype(F32))
        raw.append(jnp.sum(jnp.where(hit, scores, 0.0), axis=0, keepdims=True))
    denom = raw[0]
    for r in raw[1:]:
        denom = denom + r
    wts = [r / denom * ROUTED_SCALE for r in raw]
    pad = jnp.zeros((8 - TOP_K, n), F32)
    return jnp.concatenate(ids + [pad] + wts + [pad], axis=0), chosen


def _outproj_kernel(o_ref, x_ref, g1_ref, sc_ref, sh_ref, wout_ref, wrt_ref, br_ref, x1_ref, h2_ref, rt_ref, cnt_ref):
    bb, tt, d = x_ref.shape
    n = bb * tt
    proj = _dot(o_ref[...].reshape(n, d), wout_ref[...]).reshape(bb, tt, d)
    x1 = x_ref[...] + g1_ref[...] * proj
    x1_ref[...] = x1
    h2 = (_rms(x1) * (1.0 + sc_ref[...]) + sh_ref[...]).reshape(n, d)
    h2hi = h2.astype(BF16)
    h2_ref[...] = h2hi.reshape(bb, tt, d)
    h2lo = (h2 - h2hi.astype(F32)).astype(BF16)
    w = wrt_ref[...]
    whi = w.astype(BF16)
    wlo = (w - whi.astype(F32)).astype(BF16)
    logits_t = _dot_nt(whi, h2hi) + _dot_nt(whi, h2lo) + _dot_nt(wlo, h2hi)
    rt, chosen = _route(logits_t, br_ref[...])
    rt_ref[...] = rt
    ones = jnp.ones((8, TM), BF16)
    for j in range(n // TM):
        cnt_ref[j] = _dot_nt(ones, chosen[:, j * TM:(j + 1) * TM].astype(BF16))


def _outproj(o, x, g1, sc, sh, wout_bf16, wr_t, br_col, bb, tt):
    b, t, d = x.shape
    n = bb * tt
    steps_t = t // tt
    tok = pl.BlockSpec((bb, tt, d), lambda i, j: (i, j, 0))
    mspec = pl.BlockSpec((bb, 1, d), lambda i, j: (i, 0, 0))
    full = lambda shape: pl.BlockSpec(shape, lambda i, j: (0, 0))
    return pl.pallas_call(
        _outproj_kernel,
        out_shape=(jax.ShapeDtypeStruct((b, t, d), F32), jax.ShapeDtypeStruct((b, t, d), BF16),
                   jax.ShapeDtypeStruct((16, b * t), F32),
                   jax.ShapeDtypeStruct((b * t // TM, 8, N_EXPERTS), F32)),
        grid=(b // bb, steps_t),
        in_specs=[tok, tok, mspec, mspec, mspec, full((d, d)), full((N_EXPERTS, d)), full((N_EXPERTS, 1))],
        out_specs=(tok, tok,
                   pl.BlockSpec((16, n), lambda i, j: (0, i * steps_t + j)),
                   pl.BlockSpec((n // TM, 8, N_EXPERTS), lambda i, j: (i * steps_t + j, 0, 0))),
        compiler_params=_params("arbitrary", "arbitrary"),
        name="outproj_router",
    )(o, x, g1, sc, sh, wout_bf16, wr_t, br_col)


def _seg_chunks(seg_ref, np_ref, base_ref, g, cap, visit):
    def body(e, c):
        idx = g * N_EXPERTS + e
        s0 = seg_ref[idx]
        n16 = np_ref[idx]
        b0 = e * cap + base_ref[idx]
        for b in range(SEG_BITS):
            off = (n16 & ((1 << b) - 1)) * SEG

            @pl.when(((n16 >> b) & 1) == 1)
            def _(b=b, off=off):
                visit(pl.multiple_of(s0 + off, SEG), pl.multiple_of(b0 + off, SEG), SEG << b)
        return c
    lax.fori_loop(0, N_EXPERTS, body, 0)


def _tile_rows(seg_ref, np_ref, g):
    last = g * N_EXPERTS + N_EXPERTS - 1
    return seg_ref[last] + np_ref[last] * SEG


def _dispatch_kernel(seg_ref, np_ref, base_ref, fin_ref, rem_ref, rt_ref, h2_ref, *rest, t0, nsteps, cap, fill):
    xs_ref, pos_ref, buf, zbuf, sem, zsem = rest[-6:]
    s = pl.program_id(0)
    g = t0 + s
    slot = s % 2

    def copy(slot_, row, hrow, size):
        return pltpu.make_async_copy(buf.at[slot_, pl.ds(row, size)], xs_ref.at[pl.ds(hrow, size)], sem.at[slot_])

    def chunks(g_, fn):
        _seg_chunks(seg_ref, np_ref, base_ref, g_, cap, fn)

    @pl.when(s >= 2)
    def _():
        chunks(g - 2, lambda r, hr, n: copy(slot, r, hr, n).wait())

    rt = rt_ref[...]
    eio = lax.broadcasted_iota(jnp.int32, (N_EXPERTS, TM), 0).astype(F32)
    chosen = jnp.zeros((N_EXPERTS, TM), F32)
    for k in range(TOP_K):
        chosen = jnp.where(eio == rt[k:k + 1], 1.0, chosen)
    t_row = lax.broadcasted_iota(jnp.int32, (TM, TM), 0)
    t_col = lax.broadcasted_iota(jnp.int32, (TM, TM), 1)
    rank = _dot(chosen.astype(BF16), jnp.where(t_row < t_col, 1.0, 0.0).astype(BF16))
    n_e = jnp.sum(chosen, axis=1, keepdims=True).astype(jnp.int32)
    np16 = ((n_e + (SEG - 1)) >> 4).astype(F32)
    e_row = lax.broadcasted_iota(jnp.int32, (N_EXPERTS, N_EXPERTS), 0)
    e_col = lax.broadcasted_iota(jnp.int32, (N_EXPERTS, N_EXPERTS), 1)
    lower = jnp.where(e_col < e_row, 1.0, 0.0).astype(BF16)
    seg0 = SEG * _dot(lower, jnp.broadcast_to(np16, (N_EXPERTS, LANES)).astype(BF16))[:, 0:1]
    pos = seg0 + rank
    pos_rows = [jnp.sum(jnp.where(eio == rt[k:k + 1], pos, 0.0), axis=0, keepdims=True) for k in range(TOP_K)]
    pos_ref[...] = jnp.concatenate(pos_rows + [jnp.zeros((8 - TOP_K, TM), F32)], axis=0)

    h2 = h2_ref[...]
    nchunks = (_tile_rows(seg_ref, np_ref, g) + (CH - 1)) >> 8

    def chunk(c, carry):
        r0 = pl.multiple_of(c * CH, CH)
        rio = (lax.broadcasted_iota(jnp.int32, (CH, TM), 0) + r0).astype(F32)
        p = jnp.zeros((CH, TM), F32)
        for k in range(TOP_K):
            p = jnp.where(rio == pos_rows[k], 1.0, p)
        buf[slot, pl.ds(r0, CH), :] = _dot(p.astype(BF16), h2).astype(BF16)
        return carry
    lax.fori_loop(0, nchunks, chunk, 0)

    chunks(g, lambda r, hr, n: copy(slot, r, hr, n).start())

    @pl.when(s == nsteps - 1)
    def _():
        if nsteps >= 2:
            chunks(g - 1, lambda r, hr, n: copy(1 - slot, r, hr, n).wait())
        chunks(g, lambda r, hr, n: copy(slot, r, hr, n).wait())
        if fill:
            zbuf[...] = jnp.zeros_like(zbuf)

            def zero_chunks(fn):
                def body(e, c):
                    r16 = rem_ref[e]
                    h0 = e * cap + fin_ref[e]
                    for b in range(FILL_BITS):
                        off = (r16 & ((1 << b) - 1)) * SEG

                        @pl.when(((r16 >> b) & 1) == 1)
                        def _(b=b, off=off):
                            size = SEG << b
                            fn(pltpu.make_async_copy(zbuf.at[pl.ds(0, size)],
                                                     xs_ref.at[pl.ds(pl.multiple_of(h0 + off, SEG), size)], zsem))
                    return c
                lax.fori_loop(0, N_EXPERTS, body, 0)
            zero_chunks(lambda cp: cp.start())
            zero_chunks(lambda cp: cp.wait())


def _dispatch(plan, rt, h2, xs, t0, cap, fill):
    n, d = h2.shape
    nsteps = n // TM
    kern = functools.partial(_dispatch_kernel, t0=t0, nsteps=nsteps, cap=cap, fill=fill)
    in_specs = [pl.BlockSpec((16, TM), lambda s, *_: (0, s)), pl.BlockSpec((TM, d), lambda s, *_: (s, 0))]
    args = [rt, h2]
    aliases = {}
    if xs is not None:
        in_specs.append(pl.BlockSpec(memory_space=pl.ANY))
        args.append(xs)
        aliases = {len(plan) + 2: 0}
    return pl.pallas_call(
        kern,
        out_shape=(jax.ShapeDtypeStruct((N_EXPERTS * cap, d), BF16), jax.ShapeDtypeStruct((8, n), F32)),
        grid_spec=pltpu.PrefetchScalarGridSpec(
            num_scalar_prefetch=len(plan), grid=(nsteps,), in_specs=in_specs,
            out_specs=(pl.BlockSpec(memory_space=pl.ANY), pl.BlockSpec((8, TM), lambda s, *_: (0, s))),
            scratch_shapes=[pltpu.VMEM((2, RMAX, d), BF16), pltpu.VMEM((TR // 2, d), BF16),
                            pltpu.SemaphoreType.DMA((2,)), pltpu.SemaphoreType.DMA(())]),
        input_output_aliases=aliases,
        compiler_params=_params("arbitrary"),
        name="dispatch",
    )(*plan, *args)


def _expert_kernel(te_ref, tb_ref, tv_ref, tf_ref, xs_ref, wg_ref, wu_ref, wd_ref, ys_ref, wg16, wu16, wd16):
    i = pl.program_id(0)

    @pl.when(tf_ref[i] == 1)
    def _():
        wg16[...] = wg_ref[0].astype(BF16)
        wu16[...] = wu_ref[0].astype(BF16)
        wd16[...] = wd_ref[0].astype(BF16)

    @pl.when(tv_ref[i] == 1)
    def _():
        x = xs_ref[...]
        act = _silu(_dot(x, wg16[...])) * _dot(x, wu16[...])
        ys_ref[...] = _dot(act.astype(BF16), wd16[...]).astype(BF16)


def _experts(sched, xs, wg, wu, wd, max_tiles):
    d = xs.shape[1]
    wspec = lambda shape: pl.BlockSpec((1,) + shape, lambda i, te, tb, tv, tf: (te[i], 0, 0))
    rows = pl.BlockSpec((TR, d), lambda i, te, tb, tv, tf: (tb[i], 0))
    return pl.pallas_call(
        _expert_kernel,
        out_shape=jax.ShapeDtypeStruct(xs.shape, BF16),
        grid_spec=pltpu.PrefetchScalarGridSpec(
            num_scalar_prefetch=4, grid=(max_tiles,),
            in_specs=[rows, wspec((d, D_EXPERT)), wspec((d, D_EXPERT)), wspec((D_EXPERT, d))],
            out_specs=rows,
            scratch_shapes=[pltpu.VMEM((d, D_EXPERT), BF16), pltpu.VMEM((d, D_EXPERT), BF16),
                            pltpu.VMEM((D_EXPERT, d), BF16)]),
        compiler_params=_params("arbitrary"),
        name="experts",
    )(*sched, xs, wg, wu, wd)


def _combine_kernel(seg_ref, np_ref, base_ref, pos_ref, rt_ref, x1_ref, h2_ref, g2_ref, wsg_ref, wsu_ref, wsd_ref,
                    gf_ref, ys_ref, y_ref, ybuf, acc, sem, *, t0, nsteps, cap):
    bb, tt, d = x1_ref.shape
    s = pl.program_id(0)
    g = t0 + s
    slot = s % 2

    def copy(slot_, row, hrow, size):
        return pltpu.make_async_copy(ys_ref.at[pl.ds(hrow, size)], ybuf.at[slot_, pl.ds(row, size)], sem.at[slot_])

    def chunks(g_, fn):
        _seg_chunks(seg_ref, np_ref, base_ref, g_, cap, fn)

    @pl.when(s == 0)
    def _():
        ybuf[...] = jnp.zeros_like(ybuf)
        chunks(g, lambda r, hr, n: copy(0, r, hr, n).start())

    @pl.when(s + 1 < nsteps)
    def _():
        chunks(g + 1, lambda r, hr, n: copy(1 - slot, r, hr, n).start())

    chunks(g, lambda r, hr, n: copy(slot, r, hr, n).wait())

    stacked = jnp.concatenate([pos_ref[...], jnp.zeros((8, TM), F32), rt_ref[...],
                               jnp.zeros((LANES - 32, TM), F32)], axis=0)
    cols = stacked.T
    pos_b = [jnp.broadcast_to(cols[:, k:k + 1], (TM, CH)) for k in range(TOP_K)]
    wt_b = [jnp.broadcast_to(cols[:, 24 + k:25 + k], (TM, CH)) for k in range(TOP_K)]
    nchunks = (_tile_rows(seg_ref, np_ref, g) + (CH - 1)) >> 8
    acc[...] = jnp.zeros_like(acc)

    def chunk(c, carry):
        r0 = pl.multiple_of(c * CH, CH)
        rio = (lax.broadcasted_iota(jnp.int32, (TM, CH), 1) + r0).astype(F32)
        w = jnp.zeros((TM, CH), F32)
        for k in range(TOP_K):
            w = jnp.where(rio == pos_b[k], wt_b[k], w)
        acc[...] += _dot(w.astype(BF16), ybuf[slot, pl.ds(r0, CH), :])
        return carry
    lax.fori_loop(0, nchunks, chunk, 0)

    h = h2_ref[...].reshape(bb * tt, d)
    act = _silu(_dot(h, wsg_ref[...])) * _dot(h, wsu_ref[...])
    moe = acc[...] + _dot(act.astype(BF16), wsd_ref[...])
    x2 = x1_ref[...] + g2_ref[...] * moe.reshape(bb, tt, d)
    y_ref[...] = _rms(x2) * gf_ref[...]


def _combine(plan3, pos, rt, x1, h2, g2, wsg, wsu, wsd, gf, ys, bb, tt, t0, cap):
    b, t, d = x1.shape
    steps_t = t // tt
    nsteps = (b // bb) * steps_t
    bidx = lambda s: (s // steps_t, s % steps_t, 0)
    tok = pl.BlockSpec((bb, tt, d), lambda s, *_: bidx(s))
    full = lambda shape: pl.BlockSpec(shape, lambda s, *_: (0,) * len(shape))
    kern = functools.partial(_combine_kernel, t0=t0, nsteps=nsteps, cap=cap)
    return pl.pallas_call(
        kern,
        out_shape=jax.ShapeDtypeStruct((b, t, d), F32),
        grid_spec=pltpu.PrefetchScalarGridSpec(
            num_scalar_prefetch=3, grid=(nsteps,),
            in_specs=[pl.BlockSpec((8, TM), lambda s, *_: (0, s)), pl.BlockSpec((16, TM), lambda s, *_: (0, s)),
                      tok, tok, pl.BlockSpec((bb, 1, d), lambda s, *_: (s // steps_t, 0, 0)),
                      full(wsg.shape), full(wsu.shape), full(wsd.shape), full((1, 1, d)),
                      pl.BlockSpec(memory_space=pl.ANY)],
            out_specs=tok,
            scratch_shapes=[pltpu.VMEM((2, RMAX, d), BF16), pltpu.VMEM((TM, d), F32),
                            pltpu.SemaphoreType.DMA((2,))]),
        compiler_params=_params("arbitrary"),
        name="combine",
    )(*plan3, pos, rt, x1, h2, g2, wsg, wsu, wsd, gf, ys)


def _plan(cnt, cap, max_tiles):
    np16 = (cnt + (SEG - 1)) // SEG
    npad = np16 * SEG
    seg = jnp.cumsum(npad, axis=1) - npad
    base = jnp.cumsum(npad, axis=0) - npad
    fin = jnp.sum(npad, axis=0)
    rem16 = ((-fin) % TR) // SEG
    tiles_e = (fin + (TR - 1)) // TR
    ends = jnp.cumsum(tiles_e)
    i = jnp.arange(max_tiles, dtype=jnp.int32)
    valid = i < ends[-1]
    ii = jnp.where(valid, i, jnp.maximum(ends[-1] - 1, 0))
    te = jnp.sum((ends[None, :] <= ii[:, None]).astype(jnp.int32), axis=1)
    te = jnp.minimum(te, N_EXPERTS - 1)
    r = ii - (ends - tiles_e)[te]
    tb = te * (cap // TR) + r
    first = jnp.logical_and(valid, r == 0)
    i32 = lambda a: a.astype(jnp.int32).reshape(-1)
    return (i32(seg), i32(np16), i32(base), i32(fin), i32(rem16)), (te, i32(tb), i32(valid), i32(first))


def kernel(x_prompt, x_sample, cache_a_k, cache_a_v, cache_b_k, cache_b_v, c_prompt, c_sample, w_ada, b_ada, w_in, lam_params, g_sub_a, g_sub_b, w_out, w_router, b_router, w_gate, w_up, w_down, w_shared_gate, w_shared_up, w_shared_down, g_final):
    bp, tp, d = x_prompt.shape
    bs, ts, _ = x_sample.shape
    past = cache_a_k.shape[2]

    mod = _ada(jnp.concatenate([c_prompt, c_sample], axis=0), w_ada[0], b_ada[0])
    mod = mod.reshape(bp + bs, 6, 1, d)
    sh1, sc1, g1, sh2, sc2, g2 = (mod[:, k] for k in range(6))

    w_in16 = w_in[0].astype(BF16)
    w_out16 = w_out[0].astype(BF16)
    wr_t = w_router[0].T
    br_col = b_router[0].reshape(N_EXPERTS, 1)
    wsg16, wsu16, wsd16 = (w_shared_gate[0].astype(BF16), w_shared_up[0].astype(BF16),
                           w_shared_down[0].astype(BF16))
    ga = g_sub_a[0].reshape(1, LANES)
    gb2 = jnp.concatenate([g_sub_b[0], g_sub_b[0]]).reshape(1, LANES)
    gf = g_final.reshape(1, 1, d)
    lam_p = lam_params[0]

    def front(x, sl, bb, tt, attn):
        q, ka, va, kb, vb, kv = _inproj(x, sc1[sl], sh1[sl], w_in16, bb, tt)
        o = attn(q, kv)
        x1, h2, rt, cnt = _outproj(o, x, g1[sl], sc2[sl], sh2[sl], w_out16, wr_t, br_col, bb, tt)
        return (x1, h2, rt, cnt), (ka, va, kb, vb)

    mid_p, rows_p = front(x_prompt, slice(0, bp), 1, 512,
                          lambda q, kv: _attn_prompt(lam_p, ga, gb2, q, kv, 256))
    caches = (cache_a_k[0].reshape(bs, past, 512), cache_a_v[0].reshape(bs, past, 512),
              cache_b_k[0].reshape(bs, past, 512), cache_b_v[0].reshape(bs, past, 512))
    mid_s, rows_s = front(x_sample, slice(bp, bp + bs), 8, ts,
                          lambda q, kv: _attn_sample(lam_p, ga, gb2, q, kv, *caches))

    (x1_p, h2_p, rt_p, cnt_p), (x1_s, h2_s, rt_s, cnt_s) = mid_p, mid_s
    n_p, n_s = bp * tp, bs * ts
    tiles_p, tiles_s = n_p // TM, n_s // TM
    tiles = tiles_p + tiles_s
    cap = -(-(n_p + n_s + tiles * (SEG - 1)) // TR) * TR
    max_tiles = (TOP_K * (n_p + n_s) + tiles * N_EXPERTS * (SEG - 1)) // TR + N_EXPERTS
    cnt = jnp.concatenate([cnt_p[:, 0, :], cnt_s[:, 0, :]], axis=0).astype(jnp.int32)
    plan, sched = _plan(cnt, cap, max_tiles)
    xs, pos_p = _dispatch(plan, rt_p, h2_p.reshape(n_p, d), None, 0, cap, False)
    xs, pos_s = _dispatch(plan, rt_s, h2_s.reshape(n_s, d), xs, tiles_p, cap, True)
    ys = _experts(sched, xs, w_gate[0], w_up[0], w_down[0], max_tiles)
    y_p = _combine(plan[:3], pos_p, rt_p, x1_p, h2_p, g2[0:bp], wsg16, wsu16, wsd16, gf, ys,
                   1, TM, 0, cap)
    y_s = _combine(plan[:3], pos_s, rt_s, x1_s, h2_s, g2[bp:bp + bs], wsg16, wsu16, wsd16, gf, ys,
                   TM // ts, ts, tiles_p, cap)

    def shape_rows(rows, b, t):
        ka, va, kb, vb = rows
        return (ka.reshape(1, b, t, A_HEADS, 2 * A_QK_DIM), va.reshape(1, b, t, A_HEADS, 2 * A_QK_DIM),
                kb.reshape(1, b, t, B_HEADS, B_DIM), vb.reshape(1, b, t, B_HEADS, B_DIM))

    return (y_p, y_s) + shape_rows(rows_p, bp, tp) + shape_rows(rows_s, bs, ts)
```

```python
import functools
import math

import jax
import jax.numpy as jnp
from jax import lax
from jax.experimental import pallas as pl
from jax.experimental.pallas import tpu as pltpu

F32 = jnp.float32
BF16 = jnp.bfloat16

D_MODEL = 1024
CHUNK = 64
A_QK_DIM = 64
A_HEADS = 4
A_WIDTH = 512
B_DIM = 64
B_HEADS = 8
B_WIDTH = 512
IN_COLS = 3 * A_WIDTH + 3 * B_WIDTH
N_EXPERTS = 64
TOP_K = 6
N_GROUPS = 8
TOPK_GROUPS = 4
D_EXPERT = 256
ROUTED_SCALE = 2.5
EPS = 1e-6
NEG_INF = -1e30
LANES = 128
QK_SCALE = A_QK_DIM ** -0.5
LAM_INIT = 0.8 - 0.6 * math.exp(-0.3 * 0)
LOG2E = math.log2(math.e)
SLOPES_LOG2 = tuple(LOG2E * 2.0 ** (-8.0 * (h + 1.0) / A_HEADS) for h in range(A_HEADS))
KV_KA, KV_VA, KV_KB, KV_VB = 0, 512, 1024, 1536
VMEM_LIMIT = 56 * 1024 * 1024
TM = 256
SEG = 16
SEG_BITS = 5
CH = 256
RMAX = 2560
TR = 512
FILL_BITS = 5

_NT = (((1,), (1,)), ((), ()))


def _dot(a, b):
    return jnp.dot(a, b, preferred_element_type=F32)


def _dot_nt(a, b):
    return lax.dot_general(a, b, _NT, preferred_element_type=F32)


def _rms(x):
    return x * lax.rsqrt(jnp.mean(x * x, axis=-1, keepdims=True) + EPS)


def _silu(x):
    return x * jax.nn.sigmoid(x)


def _params(*sem):
    return pltpu.CompilerParams(dimension_semantics=sem, vmem_limit_bytes=VMEM_LIMIT)


def _ada_kernel(c_ref, w_ref, b_ref, o_ref):
    s = _silu(c_ref[...]).astype(BF16)
    o_ref[...] = _dot(s, w_ref[...].astype(BF16)) + b_ref[...]


def _ada(c, w, b):
    n, d = c.shape
    cols = w.shape[1]
    tn = 1536
    return pl.pallas_call(
        _ada_kernel,
        out_shape=jax.ShapeDtypeStruct((n, cols), F32),
        grid=(cols // tn,),
        in_specs=[pl.BlockSpec((n, d), lambda j: (0, 0)),
                  pl.BlockSpec((d, tn), lambda j: (0, j)),
                  pl.BlockSpec((1, tn), lambda j: (0, j))],
        out_specs=pl.BlockSpec((n, tn), lambda j: (0, j)),
        compiler_params=_params("arbitrary"),
        name="ada",
    )(c, w, b.reshape(1, cols))


def _inproj_kernel(x_ref, sc_ref, sh_ref, w_ref, q_ref, ka_ref, va_ref, kb_ref, vb_ref, kv_ref):
    bb, tt, d = x_ref.shape
    h = _rms(x_ref[...]) * (1.0 + sc_ref[...]) + sh_ref[...]
    proj = _dot(h.reshape(bb * tt, d).astype(BF16), w_ref[...])

    def part(lo, width):
        return proj[:, lo:lo + width].reshape(bb, tt, width)

    qa, ka, va = part(0, 512), part(512, 512), part(1024, 512)
    qb, kb, vb = part(1536, 512), part(2048, 512), part(2560, 512)
    q_ref[:, :, 0:512] = (qa * (QK_SCALE * LOG2E)).astype(BF16)
    q_ref[:, :, 512:1024] = (qb * (QK_SCALE * LOG2E)).astype(BF16)
    ka_ref[...] = ka
    va_ref[...] = va
    kb_ref[...] = kb
    vb_ref[...] = vb
    kv_ref[:, :, KV_KA:KV_KA + 512] = ka.astype(BF16)
    kv_ref[:, :, KV_VA:KV_VA + 512] = va.astype(BF16)
    kv_ref[:, :, KV_KB:KV_KB + 512] = kb.astype(BF16)
    kv_ref[:, :, KV_VB:KV_VB + 512] = vb.astype(BF16)


def _inproj(x, sc, sh, w_bf16, bb, tt):
    b, t, d = x.shape
    xspec = pl.BlockSpec((bb, tt, d), lambda i, j: (i, j, 0))
    mspec = pl.BlockSpec((bb, 1, d), lambda i, j: (i, 0, 0))

    def ospec(width):
        return pl.BlockSpec((bb, tt, width), lambda i, j: (i, j, 0))

    f32rows = jax.ShapeDtypeStruct((b, t, 512), F32)
    return pl.pallas_call(
        _inproj_kernel,
        out_shape=(jax.ShapeDtypeStruct((b, t, 1024), BF16), f32rows, f32rows, f32rows, f32rows,
                   jax.ShapeDtypeStruct((b, t, 2048), BF16)),
        grid=(b // bb, t // tt),
        in_specs=[xspec, mspec, mspec, pl.BlockSpec((d, IN_COLS), lambda i, j: (0, 0))],
        out_specs=(ospec(1024), ospec(512), ospec(512), ospec(512), ospec(512), ospec(2048)),
        compiler_params=_params("arbitrary", "arbitrary"),
        name="inproj",
    )(x, sc, sh, w_bf16)


def _lane_lo(shape):
    return lax.broadcasted_iota(jnp.int32, shape, len(shape) - 1) < 64


def _split_halves(x):
    lo = _lane_lo(x.shape)
    zero = jnp.zeros_like(x)
    return jnp.concatenate([jnp.where(lo, x, zero), jnp.where(lo, zero, x)], axis=0)


def _lam_of(lam_ref):
    lp = lam_ref[...]
    s1 = jnp.sum(lp[0:1] * lp[1:2], axis=1, keepdims=True)
    s2 = jnp.sum(lp[2:3] * lp[3:4], axis=1, keepdims=True)
    return jnp.exp(s1) - jnp.exp(s2) + LAM_INIT


def _a_update(state, s, v16):
    m, l, acc = state
    m_new = jnp.maximum(m, jnp.max(s, axis=1, keepdims=True))
    alpha = jnp.exp2(m - m_new)
    p = jnp.exp2(s - m_new)
    l = alpha * l + jnp.sum(p, axis=1, keepdims=True)
    acc = alpha * acc + _dot(p.astype(BF16), v16)
    return m_new, l, acc


def _a_init(rows):
    return (jnp.full((rows, 1), NEG_INF, F32), jnp.zeros((rows, 1), F32), jnp.zeros((rows, LANES), F32))


def _a_finish(state, lam, gain, nq):
    _, l, acc = state
    o = acc[:nq] / l[:nq] - lam * (acc[nq:] / l[nq:])
    return _rms(o) * gain * (1.0 - LAM_INIT)


def _diag_tiles(nq, nk):
    r = lax.broadcasted_iota(jnp.int32, (2 * nq, nk), 0)
    r = jnp.where(r >= nq, r - nq, r)
    c = lax.broadcasted_iota(jnp.int32, (2 * nq, nk), 1)
    visible = (c // CHUNK) <= (r // CHUNK)
    before = c < r
    dist_shift = (r - jnp.abs(r - c)).astype(F32)
    return visible, before, dist_shift


def _strict_upper(n):
    j = lax.broadcasted_iota(jnp.int32, (n, n), 0)
    s = lax.broadcasted_iota(jnp.int32, (n, n), 1)
    return jnp.where(j > s, 1.0, 0.0).astype(BF16)


def _b_block(z, v16, tri, carry, acc, before, nq):
    sp = jnp.where(z > 60.0, z, jnp.log2(1.0 + jnp.exp2(z)))
    if before is not None:
        sp = jnp.where(before, sp, 0.0)
    later = _dot(sp.astype(BF16), tri)
    a = jnp.exp2((z - sp) - (carry + later))
    if before is not None:
        a = jnp.where(before, a, 0.0)
    a16 = a.astype(BF16)
    vlo = _lane_lo(v16.shape)
    vzero = jnp.zeros_like(v16)
    acc = acc + _dot(a16[:nq], jnp.where(vlo, v16, vzero)) + _dot(a16[nq:], jnp.where(vlo, vzero, v16))
    carry = carry + jnp.sum(sp, axis=1, keepdims=True)
    return carry, acc


def _b_finish(acc, gain2):
    lo = _lane_lo(acc.shape)
    ss = acc * acc
    s_lo = jnp.sum(jnp.where(lo, ss, 0.0), axis=1, keepdims=True)
    s_hi = jnp.sum(jnp.where(lo, 0.0, ss), axis=1, keepdims=True)
    ms = jnp.where(lo, s_lo, s_hi) * (1.0 / B_DIM)
    return acc * lax.rsqrt(ms + EPS) * gain2


def _attn_prompt_kernel(lam_ref, ga_ref, gb_ref, q_ref, kv_ref, o_ref):
    bq = q_ref.shape[1]
    pairs = B_HEADS // 2
    i = pl.program_id(1)
    q0 = pl.multiple_of(i * bq, bq)
    lam = _lam_of(lam_ref)
    visible, before, dist_shift = _diag_tiles(bq, bq)
    col = lax.broadcasted_iota(jnp.int32, (1, bq), 1)
    tri = _strict_upper(bq)
    qa = [_split_halves(q_ref[0, :, h * LANES:(h + 1) * LANES]) for h in range(A_HEADS)]
    qb = [_split_halves(q_ref[0, :, A_WIDTH + p * LANES:A_WIDTH + (p + 1) * LANES]) for p in range(pairs)]

    def kv(rows, base, n):
        return kv_ref[0, rows, base + n * LANES:base + (n + 1) * LANES]

    rows = pl.ds(q0, bq)
    a_st = tuple(
        _a_update(_a_init(2 * bq),
                  jnp.where(visible, _dot_nt(qa[h], kv(rows, KV_KA, h)) + SLOPES_LOG2[h] * dist_shift, NEG_INF),
                  kv(rows, KV_VA, h))
        for h in range(A_HEADS))
    b_st = tuple(
        _b_block(_dot_nt(qb[p], kv(rows, KV_KB, p)), kv(rows, KV_VB, p), tri,
                 jnp.zeros((2 * bq, 1), F32), jnp.zeros((bq, LANES), F32), before, bq)
        for p in range(pairs))

    def body(jj, st):
        a_st, b_st = st
        k0 = pl.multiple_of((i - 1 - jj) * bq, bq)
        rows = pl.ds(k0, bq)
        rel = (col + (k0 - q0)).astype(F32)
        a_new = tuple(
            _a_update(a_st[h], _dot_nt(qa[h], kv(rows, KV_KA, h)) + SLOPES_LOG2[h] * rel, kv(rows, KV_VA, h))
            for h in range(A_HEADS))
        b_new = tuple(
            _b_block(_dot_nt(qb[p], kv(rows, KV_KB, p)), kv(rows, KV_VB, p), tri, b_st[p][0], b_st[p][1], None, bq)
            for p in range(pairs))
        return a_new, b_new

    a_st, b_st = lax.fori_loop(0, i, body, (a_st, b_st))
    for h in range(A_HEADS):
        o_ref[0, :, h * LANES:(h + 1) * LANES] = _a_finish(a_st[h], lam, ga_ref[...], bq).astype(o_ref.dtype)
    for p in range(pairs):
        o_ref[0, :, A_WIDTH + p * LANES:A_WIDTH + (p + 1) * LANES] = (
            _b_finish(b_st[p][1], gb_ref[...]).astype(o_ref.dtype))


def _attn_prompt(lam_p, ga, gb2, q, kv, bq):
    b, t, _ = q.shape
    small = lambda shape: pl.BlockSpec(shape, lambda bi, i: (0, 0))
    return pl.pallas_call(
        _attn_prompt_kernel,
        out_shape=jax.ShapeDtypeStruct((b, t, D_MODEL), BF16),
        grid=(b, t // bq),
        in_specs=[small((4, A_QK_DIM)), small((1, LANES)), small((1, LANES)),
                  pl.BlockSpec((1, bq, D_MODEL), lambda bi, i: (bi, i, 0)),
                  pl.BlockSpec((1, t, 2048), lambda bi, i: (bi, 0, 0))],
        out_specs=pl.BlockSpec((1, bq, D_MODEL), lambda bi, i: (bi, i, 0)),
        compiler_params=_params("arbitrary", "arbitrary"),
        name="attn_prompt",
    )(lam_p, ga, gb2, q, kv)


def _attn_sample_kernel(lam_ref, ga_ref, gb_ref, q_ref, kv_ref, cak_ref, cav_ref, cbk_ref, cbv_ref, o_ref):
    nq = q_ref.shape[1]
    past = cak_ref.shape[1]
    pb = 256
    lam = _lam_of(lam_ref)
    visible, before, dist_shift = _diag_tiles(nq, nq)
    pcol = lax.broadcasted_iota(jnp.int32, (1, past), 1)
    tri_new = _strict_upper(nq)
    tri_past = _strict_upper(pb)

    for h in range(A_HEADS):
        cs = slice(h * LANES, (h + 1) * LANES)
        qs = _split_halves(q_ref[0, :, cs])
        slope = SLOPES_LOG2[h]
        s_past = _dot_nt(qs, cak_ref[0, :, cs].astype(BF16)) + slope * (pcol - past).astype(F32)
        state = _a_update(_a_init(2 * nq), s_past, cav_ref[0, :, cs].astype(BF16))
        k16 = kv_ref[0, :, KV_KA + h * LANES:KV_KA + (h + 1) * LANES]
        v16 = kv_ref[0, :, KV_VA + h * LANES:KV_VA + (h + 1) * LANES]
        s_new = jnp.where(visible, _dot_nt(qs, k16) + slope * dist_shift, NEG_INF)
        state = _a_update(state, s_new, v16)
        o_ref[0, :, cs] = _a_finish(state, lam, ga_ref[...], nq).astype(o_ref.dtype)

    for p in range(B_HEADS // 2):
        cs = slice(p * LANES, (p + 1) * LANES)
        qs = _split_halves(q_ref[0, :, A_WIDTH + p * LANES:A_WIDTH + (p + 1) * LANES])
        k16 = kv_ref[0, :, KV_KB + p * LANES:KV_KB + (p + 1) * LANES]
        v16 = kv_ref[0, :, KV_VB + p * LANES:KV_VB + (p + 1) * LANES]
        carry, acc = _b_block(_dot_nt(qs, k16), v16, tri_new,
                              jnp.zeros((2 * nq, 1), F32), jnp.zeros((nq, LANES), F32), before, nq)
        for j in reversed(range(past // pb)):
            rows = slice(j * pb, (j + 1) * pb)
            z = _dot_nt(qs, cbk_ref[0, rows, cs].astype(BF16))
            carry, acc = _b_block(z, cbv_ref[0, rows, cs].astype(BF16), tri_past, carry, acc, None, nq)
        o_ref[0, :, A_WIDTH + p * LANES:A_WIDTH + (p + 1) * LANES] = _b_finish(acc, gb_ref[...]).astype(o_ref.dtype)


def _attn_sample(lam_p, ga, gb2, q, kv, cak, cav, cbk, cbv):
    b, t, _ = q.shape
    past = cak.shape[1]
    small = lambda shape: pl.BlockSpec(shape, lambda bi: (0, 0))
    per_b = lambda rows, width: pl.BlockSpec((1, rows, width), lambda bi: (bi, 0, 0))
    return pl.pallas_call(
        _attn_sample_kernel,
        out_shape=jax.ShapeDtypeStruct((b, t, D_MODEL), BF16),
        grid=(b,),
        in_specs=[small((4, A_QK_DIM)), small((1, LANES)), small((1, LANES)),
                  per_b(t, D_MODEL), per_b(t, 2048),
                  per_b(past, 512), per_b(past, 512), per_b(past, 512), per_b(past, 512)],
        out_specs=per_b(t, D_MODEL),
        compiler_params=_params("arbitrary"),
        name="attn_sample",
    )(lam_p, ga, gb2, q, kv, cak, cav, cbk, cbv)


def _route(logits_t, bias_col):
    n = logits_t.shape[1]
    scores = jax.nn.sigmoid(logits_t)
    sel3 = (scores + bias_col).reshape(N_GROUPS, 8, n)
    eio = lax.broadcasted_iota(jnp.int32, (N_GROUPS, 8, n), 1)
    m1 = jnp.max(sel3, axis=1, keepdims=True)
    i1 = jnp.min(jnp.where(sel3 == m1, eio, 8), axis=1, keepdims=True)
    m2 = jnp.max(jnp.where(eio == i1, -jnp.inf, sel3), axis=1, keepdims=True)
    gscore = (m1 + m2).reshape(N_GROUPS, n)
    gio = lax.broadcasted_iota(jnp.int32, (N_GROUPS, n), 0)
    keep = jnp.zeros((N_GROUPS, n), F32)
    for _ in range(TOPK_GROUPS):
        g = jnp.max(gscore, axis=0, keepdims=True)
        gi = jnp.min(jnp.where(gscore == g, gio, N_GROUPS), axis=0, keepdims=True)
        hit = gio == gi
        keep = jnp.where(hit, 1.0, keep)
        gscore = jnp.where(hit, -jnp.inf, gscore)
    cur = jnp.where(keep.reshape(N_GROUPS, 1, n) > 0.0, sel3, NEG_INF).reshape(N_EXPERTS, n)
    xio = lax.broadcasted_iota(jnp.int32, (N_EXPERTS, n), 0)
    chosen = jnp.zeros((N_EXPERTS, n), F32)
    ids, raw = [], []
    for _ in range(TOP_K):
        mx = jnp.max(cur, axis=0, keepdims=True)
        ei = jnp.min(jnp.where(cur == mx, xio, N_EXPERTS), axis=0, keepdims=True)
        hit = xio == ei
        chosen = jnp.where(hit, 1.0, chosen)
        cur = jnp.where(hit, -jnp.inf, cur)
        ids.append(ei.astype(F32))
        raw.append(jnp.sum(jnp.where(hit, scores, 0.0), axis=0, keepdims=True))
    denom = raw[0]
    for r in raw[1:]:
        denom = denom + r
    wts = [r / denom * ROUTED_SCALE for r in raw]
    pad = jnp.zeros((8 - TOP_K, n), F32)
    return jnp.concatenate(ids + [pad] + wts + [pad], axis=0), chosen


def _outproj_kernel(o_ref, x_ref, g1_ref, sc_ref, sh_ref, wout_ref, wrt_ref, br_ref, x1_ref, h2_ref, rt_ref, cnt_ref):
    bb, tt, d = x_ref.shape
    n = bb * tt
    proj = _dot(o_ref[...].reshape(n, d), wout_ref[...]).reshape(bb, tt, d)
    x1 = x_ref[...] + g1_ref[...] * proj
    x1_ref[...] = x1
    h2 = (_rms(x1) * (1.0 + sc_ref[...]) + sh_ref[...]).reshape(n, d)
    h2hi = h2.astype(BF16)
    h2_ref[...] = h2hi.reshape(bb, tt, d)
    h2lo = (h2 - h2hi.astype(F32)).astype(BF16)
    w = wrt_ref[...]
    whi = w.astype(BF16)
    wlo = (w - whi.astype(F32)).astype(BF16)
    logits_t = _dot_nt(whi, h2hi) + _dot_nt(whi, h2lo) + _dot_nt(wlo, h2hi)
    rt, chosen = _route(logits_t, br_ref[...])
    rt_ref[...] = rt
    ones = jnp.ones((8, TM), BF16)
    for j in range(n // TM):
        cnt_ref[j] = _dot_nt(ones, chosen[:, j * TM:(j + 1) * TM].astype(BF16))


def _outproj(o, x, g1, sc, sh, wout_bf16, wr_t, br_col, bb, tt):
    b, t, d = x.shape
    n = bb * tt
    steps_t = t // tt
    tok = pl.BlockSpec((bb, tt, d), lambda i, j: (i, j, 0))
    mspec = pl.BlockSpec((bb, 1, d), lambda i, j: (i, 0, 0))
    full = lambda shape: pl.BlockSpec(shape, lambda i, j: (0, 0))
    return pl.pallas_call(
        _outproj_kernel,
        out_shape=(jax.ShapeDtypeStruct((b, t, d), F32), jax.ShapeDtypeStruct((b, t, d), BF16),
                   jax.ShapeDtypeStruct((16, b * t), F32),
                   jax.ShapeDtypeStruct((b * t // TM, 8, N_EXPERTS), F32)),
        grid=(b // bb, steps_t),
        in_specs=[tok, tok, mspec, mspec, mspec, full((d, d)), full((N_EXPERTS, d)), full((N_EXPERTS, 1))],
        out_specs=(tok, tok,
                   pl.BlockSpec((16, n), lambda i, j: (0, i * steps_t + j)),
                   pl.BlockSpec((n // TM, 8, N_EXPERTS), lambda i, j: (i * steps_t + j, 0, 0))),
        compiler_params=_params("arbitrary", "arbitrary"),
        name="outproj_router",
    )(o, x, g1, sc, sh, wout_bf16, wr_t, br_col)


def _seg_chunks(seg_ref, np_ref, base_ref, g, cap, visit):
    def body(e, c):
        idx = g * N_EXPERTS + e
        s0 = seg_ref[idx]
        n16 = np_ref[idx]
        b0 = e * cap + base_ref[idx]
        for b in range(SEG_BITS):
            off = (n16 & ((1 << b) - 1)) * SEG

            @pl.when(((n16 >> b) & 1) == 1)
            def _(b=b, off=off):
                visit(pl.multiple_of(s0 + off, SEG), pl.multiple_of(b0 + off, SEG), SEG << b)
        return c
    lax.fori_loop(0, N_EXPERTS, body, 0)


def _tile_rows(seg_ref, np_ref, g):
    last = g * N_EXPERTS + N_EXPERTS - 1
    return seg_ref[last] + np_ref[last] * SEG


def _dispatch_kernel(seg_ref, np_ref, base_ref, fin_ref, rem_ref, rt_ref, h2_ref, *rest, t0, nsteps, cap, fill):
    xs_ref, pos_ref, buf, zbuf, sem, zsem = rest[-6:]
    s = pl.program_id(0)
    g = t0 + s
    slot = s % 2

    def copy(slot_, row, hrow, size):
        return pltpu.make_async_copy(buf.at[slot_, pl.ds(row, size)], xs_ref.at[pl.ds(hrow, size)], sem.at[slot_])

    def chunks(g_, fn):
        _seg_chunks(seg_ref, np_ref, base_ref, g_, cap, fn)

    @pl.when(s >= 2)
    def _():
        chunks(g - 2, lambda r, hr, n: copy(slot, r, hr, n).wait())

    rt = rt_ref[...]
    eio = lax.broadcasted_iota(jnp.int32, (N_EXPERTS, TM), 0).astype(F32)
    chosen = jnp.zeros((N_EXPERTS, TM), F32)
    for k in range(TOP_K):
        chosen = jnp.where(eio == rt[k:k + 1], 1.0, chosen)
    t_row = lax.broadcasted_iota(jnp.int32, (TM, TM), 0)
    t_col = lax.broadcasted_iota(jnp.int32, (TM, TM), 1)
    rank = _dot(chosen.astype(BF16), jnp.where(t_row < t_col, 1.0, 0.0).astype(BF16))
    n_e = jnp.sum(chosen, axis=1, keepdims=True).astype(jnp.int32)
    np16 = ((n_e + (SEG - 1)) >> 4).astype(F32)
    e_row = lax.broadcasted_iota(jnp.int32, (N_EXPERTS, N_EXPERTS), 0)
    e_col = lax.broadcasted_iota(jnp.int32, (N_EXPERTS, N_EXPERTS), 1)
    lower = jnp.where(e_col < e_row, 1.0, 0.0).astype(BF16)
    seg0 = SEG * _dot(lower, jnp.broadcast_to(np16, (N_EXPERTS, LANES)).astype(BF16))[:, 0:1]
    pos = seg0 + rank
    pos_rows = [jnp.sum(jnp.where(eio == rt[k:k + 1], pos, 0.0), axis=0, keepdims=True) for k in range(TOP_K)]
    pos_ref[...] = jnp.concatenate(pos_rows + [jnp.zeros((8 - TOP_K, TM), F32)], axis=0)

    h2 = h2_ref[...]
    nchunks = (_tile_rows(seg_ref, np_ref, g) + (CH - 1)) >> 8

    def chunk(c, carry):
        r0 = pl.multiple_of(c * CH, CH)
        rio = (lax.broadcasted_iota(jnp.int32, (CH, TM), 0) + r0).astype(F32)
        p = jnp.zeros((CH, TM), F32)
        for k in range(TOP_K):
            p = jnp.where(rio == pos_rows[k], 1.0, p)
        buf[slot, pl.ds(r0, CH), :] = _dot(p.astype(BF16), h2).astype(BF16)
        return carry
    lax.fori_loop(0, nchunks, chunk, 0)

    chunks(g, lambda r, hr, n: copy(slot, r, hr, n).start())

    @pl.when(s == nsteps - 1)
    def _():
        if nsteps >= 2:
            chunks(g - 1, lambda r, hr, n: copy(1 - slot, r, hr, n).wait())
        chunks(g, lambda r, hr, n: copy(slot, r, hr, n).wait())
        if fill:
            zbuf[...] = jnp.zeros_like(zbuf)

            def zero_chunks(fn):
                def body(e, c):
                    r16 = rem_ref[e]
                    h0 = e * cap + fin_ref[e]
                    for b in range(FILL_BITS):
                        off = (r16 & ((1 << b) - 1)) * SEG

                        @pl.when(((r16 >> b) & 1) == 1)
                        def _(b=b, off=off):
                            size = SEG << b
                            fn(pltpu.make_async_copy(zbuf.at[pl.ds(0, size)],
                                                     xs_ref.at[pl.ds(pl.multiple_of(h0 + off, SEG), size)], zsem))
                    return c
                lax.fori_loop(0, N_EXPERTS, body, 0)
            zero_chunks(lambda cp: cp.start())
            zero_chunks(lambda cp: cp.wait())


def _dispatch(plan, rt, h2, xs, t0, cap, fill):
    n, d = h2.shape
    nsteps = n // TM
    kern = functools.partial(_dispatch_kernel, t0=t0, nsteps=nsteps, cap=cap, fill=fill)
    in_specs = [pl.BlockSpec((16, TM), lambda s, *_: (0, s)), pl.BlockSpec((TM, d), lambda s, *_: (s, 0))]
    args = [rt, h2]
    aliases = {}
    if xs is not None:
        in_specs.append(pl.BlockSpec(memory_space=pl.ANY))
        args.append(xs)
        aliases = {len(plan) + 2: 0}
    return pl.pallas_call(
        kern,
        out_shape=(jax.ShapeDtypeStruct((N_EXPERTS * cap, d), BF16), jax.ShapeDtypeStruct((8, n), F32)),
        grid_spec=pltpu.PrefetchScalarGridSpec(
            num_scalar_prefetch=len(plan), grid=(nsteps,), in_specs=in_specs,
            out_specs=(pl.BlockSpec(memory_space=pl.ANY), pl.BlockSpec((8, TM), lambda s, *_: (0, s))),
            scratch_shapes=[pltpu.VMEM((2, RMAX, d), BF16), pltpu.VMEM((TR // 2, d), BF16),
                            pltpu.SemaphoreType.DMA((2,)), pltpu.SemaphoreType.DMA(())]),
        input_output_aliases=aliases,
        compiler_params=_params("arbitrary"),
        name="dispatch",
    )(*plan, *args)


def _expert_kernel(te_ref, tb_ref, tv_ref, tf_ref, xs_ref, wg_ref, wu_ref, wd_ref, ys_ref, wg16, wu16, wd16):
    i = pl.program_id(0)

    @pl.when(tf_ref[i] == 1)
    def _():
        wg16[...] = wg_ref[0].astype(BF16)
        wu16[...] = wu_ref[0].astype(BF16)
        wd16[...] = wd_ref[0].astype(BF16)

    @pl.when(tv_ref[i] == 1)
    def _():
        x = xs_ref[...]
        act = _silu(_dot(x, wg16[...])) * _dot(x, wu16[...])
        ys_ref[...] = _dot(act.astype(BF16), wd16[...]).astype(BF16)


def _experts(sched, xs, wg, wu, wd, max_tiles):
    d = xs.shape[1]
    wspec = lambda shape: pl.BlockSpec((1,) + shape, lambda i, te, tb, tv, tf: (te[i], 0, 0))
    rows = pl.BlockSpec((TR, d), lambda i, te, tb, tv, tf: (tb[i], 0))
    return pl.pallas_call(
        _expert_kernel,
        out_shape=jax.ShapeDtypeStruct(xs.shape, BF16),
        grid_spec=pltpu.PrefetchScalarGridSpec(
            num_scalar_prefetch=4, grid=(max_tiles,),
            in_specs=[rows, wspec((d, D_EXPERT)), wspec((d, D_EXPERT)), wspec((D_EXPERT, d))],
            out_specs=rows,
            scratch_shapes=[pltpu.VMEM((d, D_EXPERT), BF16), pltpu.VMEM((d, D_EXPERT), BF16),
                            pltpu.VMEM((D_EXPERT, d), BF16)]),
        compiler_params=_params("arbitrary"),
        name="experts",
    )(*sched, xs, wg, wu, wd)


def _combine_kernel(seg_ref, np_ref, base_ref, pos_ref, rt_ref, x1_ref, h2_ref, g2_ref, wsg_ref, wsu_ref, wsd_ref,
                    gf_ref, ys_ref, y_ref, ybuf, acc, sem, *, t0, nsteps, cap):
    bb, tt, d = x1_ref.shape
    s = pl.program_id(0)
    g = t0 + s
    slot = s % 2

    def copy(slot_, row, hrow, size):
        return pltpu.make_async_copy(ys_ref.at[pl.ds(hrow, size)], ybuf.at[slot_, pl.ds(row, size)], sem.at[slot_])

    def chunks(g_, fn):
        _seg_chunks(seg_ref, np_ref, base_ref, g_, cap, fn)

    @pl.when(s == 0)
    def _():
        ybuf[...] = jnp.zeros_like(ybuf)
        chunks(g, lambda r, hr, n: copy(0, r, hr, n).start())

    @pl.when(s + 1 < nsteps)
    def _():
        chunks(g + 1, lambda r, hr, n: copy(1 - slot, r, hr, n).start())

    chunks(g, lambda r, hr, n: copy(slot, r, hr, n).wait())

    stacked = jnp.concatenate([pos_ref[...], jnp.zeros((8, TM), F32), rt_ref[...],
                               jnp.zeros((LANES - 32, TM), F32)], axis=0)
    cols = stacked.T
    pos_b = [jnp.broadcast_to(cols[:, k:k + 1], (TM, CH)) for k in range(TOP_K)]
    wt_b = [jnp.broadcast_to(cols[:, 24 + k:25 + k], (TM, CH)) for k in range(TOP_K)]
    nchunks = (_tile_rows(seg_ref, np_ref, g) + (CH - 1)) >> 8
    acc[...] = jnp.zeros_like(acc)

    def chunk(c, carry):
        r0 = pl.multiple_of(c * CH, CH)
        rio = (lax.broadcasted_iota(jnp.int32, (TM, CH), 1) + r0).astype(F32)
        w = jnp.zeros((TM, CH), F32)
        for k in range(TOP_K):
            w = jnp.where(rio == pos_b[k], wt_b[k], w)
        acc[...] += _dot(w.astype(BF16), ybuf[slot, pl.ds(r0, CH), :])
        return carry
    lax.fori_loop(0, nchunks, chunk, 0)

    h = h2_ref[...].reshape(bb * tt, d)
    act = _silu(_dot(h, wsg_ref[...])) * _dot(h, wsu_ref[...])
    moe = acc[...] + _dot(act.astype(BF16), wsd_ref[...])
    x2 = x1_ref[...] + g2_ref[...] * moe.reshape(bb, tt, d)
    y_ref[...] = _rms(x2) * gf_ref[...]


def _combine(plan3, pos, rt, x1, h2, g2, wsg, wsu, wsd, gf, ys, bb, tt, t0, cap):
    b, t, d = x1.shape
    steps_t = t // tt
    nsteps = (b // bb) * steps_t
    bidx = lambda s: (s // steps_t, s % steps_t, 0)
    tok = pl.BlockSpec((bb, tt, d), lambda s, *_: bidx(s))
    full = lambda shape: pl.BlockSpec(shape, lambda s, *_: (0,) * len(shape))
    kern = functools.partial(_combine_kernel, t0=t0, nsteps=nsteps, cap=cap)
    return pl.pallas_call(
        kern,
        out_shape=jax.ShapeDtypeStruct((b, t, d), F32),
        grid_spec=pltpu.PrefetchScalarGridSpec(
            num_scalar_prefetch=3, grid=(nsteps,),
            in_specs=[pl.BlockSpec((8, TM), lambda s, *_: (0, s)), pl.BlockSpec((16, TM), lambda s, *_: (0, s)),
                      tok, tok, pl.BlockSpec((bb, 1, d), lambda s, *_: (s // steps_t, 0, 0)),
                      full(wsg.shape), full(wsu.shape), full(wsd.shape), full((1, 1, d)),
                      pl.BlockSpec(memory_space=pl.ANY)],
            out_specs=tok,
            scratch_shapes=[pltpu.VMEM((2, RMAX, d), BF16), pltpu.VMEM((TM, d), F32),
                            pltpu.SemaphoreType.DMA((2,))]),
        compiler_params=_params("arbitrary"),
        name="combine",
    )(*plan3, pos, rt, x1, h2, g2, wsg, wsu, wsd, gf, ys)


def _plan(cnt, cap, max_tiles):
    np16 = (cnt + (SEG - 1)) // SEG
    npad = np16 * SEG
    seg = jnp.cumsum(npad, axis=1) - npad
    base = jnp.cumsum(npad, axis=0) - npad
    fin = jnp.sum(npad, axis=0)
    rem16 = ((-fin) % TR) // SEG
    tiles_e = (fin + (TR - 1)) // TR
    ends = jnp.cumsum(tiles_e)
    i = jnp.arange(max_tiles, dtype=jnp.int32)
    valid = i < ends[-1]
    ii = jnp.where(valid, i, jnp.maximum(ends[-1] - 1, 0))
    te = jnp.sum((ends[None, :] <= ii[:, None]).astype(jnp.int32), axis=1)
    te = jnp.minimum(te, N_EXPERTS - 1)
    r = ii - (ends - tiles_e)[te]
    tb = te * (cap // TR) + r
    first = jnp.logical_and(valid, r == 0)
    i32 = lambda a: a.astype(jnp.int32).reshape(-1)
    return (i32(seg), i32(np16), i32(base), i32(fin), i32(rem16)), (te, i32(tb), i32(valid), i32(first))


def kernel(x_prompt, x_sample, cache_a_k, cache_a_v, cache_b_k, cache_b_v, c_prompt, c_sample, w_ada, b_ada, w_in, lam_params, g_sub_a, g_sub_b, w_out, w_router, b_router, w_gate, w_up, w_down, w_shared_gate, w_shared_up, w_shared_down, g_final):
    bp, tp, d = x_prompt.shape
    bs, ts, _ = x_sample.shape
    past = cache_a_k.shape[2]

    mod = _ada(jnp.concatenate([c_prompt, c_sample], axis=0), w_ada[0], b_ada[0])
    mod = mod.reshape(bp + bs, 6, 1, d)
    sh1, sc1, g1, sh2, sc2, g2 = (mod[:, k] for k in range(6))

    w_in16 = w_in[0].astype(BF16)
    w_out16 = w_out[0].astype(BF16)
    wr_t = w_router[0].T
    br_col = b_router[0].reshape(N_EXPERTS, 1)
    wsg16, wsu16, wsd16 = (w_shared_gate[0].astype(BF16), w_shared_up[0].astype(BF16),
                           w_shared_down[0].astype(BF16))
    ga = g_sub_a[0].reshape(1, LANES)
    gb2 = jnp.concatenate([g_sub_b[0], g_sub_b[0]]).reshape(1, LANES)
    gf = g_final.reshape(1, 1, d)
    lam_p = lam_params[0]

    def front(x, sl, bb, tt, attn):
        q, ka, va, kb, vb, kv = _inproj(x, sc1[sl], sh1[sl], w_in16, bb, tt)
        o = attn(q, kv)
        x1, h2, rt, cnt = _outproj(o, x, g1[sl], sc2[sl], sh2[sl], w_out16, wr_t, br_col, bb, tt)
        return (x1, h2, rt, cnt), (ka, va, kb, vb)

    mid_p, rows_p = front(x_prompt, slice(0, bp), 1, 512,
                          lambda q, kv: _attn_prompt(lam_p, ga, gb2, q, kv, 256))
    caches = (cache_a_k[0].reshape(bs, past, 512), cache_a_v[0].reshape(bs, past, 512),
              cache_b_k[0].reshape(bs, past, 512), cache_b_v[0].reshape(bs, past, 512))
    mid_s, rows_s = front(x_sample, slice(bp, bp + bs), 8, ts,
                          lambda q, kv: _attn_sample(lam_p, ga, gb2, q, kv, *caches))

    (x1_p, h2_p, rt_p, cnt_p), (x1_s, h2_s, rt_s, cnt_s) = mid_p, mid_s
    n_p, n_s = bp * tp, bs * ts
    tiles_p, tiles_s = n_p // TM, n_s // TM
    tiles = tiles_p + tiles_s
    cap = -(-(n_p + n_s + tiles * (SEG - 1)) // TR) * TR
    max_tiles = (TOP_K * (n_p + n_s) + tiles * N_EXPERTS * (SEG - 1)) // TR + N_EXPERTS
    cnt = jnp.concatenate([cnt_p[:, 0, :], cnt_s[:, 0, :]], axis=0).astype(jnp.int32)
    plan, sched = _plan(cnt, cap, max_tiles)
    xs, pos_p = _dispatch(plan, rt_p, h2_p.reshape(n_p, d), None, 0, cap, False)
    xs, pos_s = _dispatch(plan, rt_s, h2_s.reshape(n_s, d), xs, tiles_p, cap, True)
    ys = _experts(sched, xs, w_gate[0], w_up[0], w_down[0], max_tiles)
    y_p = _combine(plan[:3], pos_p, rt_p, x1_p, h2_p, g2[0:bp], wsg16, wsu16, wsd16, gf, ys,
                   1, TM, 0, cap)
    y_s = _combine(plan[:3], pos_s, rt_s, x1_s, h2_s, g2[bp:bp + bs], wsg16, wsu16, wsd16, gf, ys,
                   TM // ts, ts, tiles_p, cap)

    def shape_rows(rows, b, t):
        ka, va, kb, vb = rows
        return (ka.reshape(1, b, t, A_HEADS, 2 * A_QK_DIM), va.reshape(1, b, t, A_HEADS, 2 * A_QK_DIM),
                kb.reshape(1, b, t, B_HEADS, B_DIM), vb.reshape(1, b, t, B_HEADS, B_DIM))

    return (y_p, y_s) + shape_rows(rows_p, bp, tp) + shape_rows(rows_s, bs, ts)
```

```python
import functools
import math

import jax
import jax.numpy as jnp
from jax import lax
from jax.experimental import pallas as pl
from jax.experimental.pallas import tpu as pltpu

F32 = jnp.float32
BF16 = jnp.bfloat16

D_MODEL = 1024
CHUNK = 64
A_QK_DIM = 64
A_HEADS = 4
A_WIDTH = 512
B_DIM = 64
B_HEADS = 8
B_WIDTH = 512
IN_COLS = 3 * A_WIDTH + 3 * B_WIDTH
N_EXPERTS = 64
TOP_K = 6
N_GROUPS = 8
TOPK_GROUPS = 4
D_EXPERT = 256
ROUTED_SCALE = 2.5
EPS = 1e-6
NEG_INF = -1e30
LANES = 128
QK_SCALE = A_QK_DIM ** -0.5
LAM_INIT = 0.8 - 0.6 * math.exp(-0.3 * 0)
LOG2E = math.log2(math.e)
SLOPES_LOG2 = tuple(LOG2E * 2.0 ** (-8.0 * (h + 1.0) / A_HEADS) for h in range(A_HEADS))
KV_KA, KV_VA, KV_KB, KV_VB = 0, 512, 1024, 1536
VMEM_LIMIT = 56 * 1024 * 1024
TM = 256
SEG_SHIFT = 4
SEG = 1 << SEG_SHIFT
SEG_BITS = 5
CH_SHIFT = 8
CH = 1 << CH_SHIFT
RMAX = 2560
TILE_BITS = 8
TR = 512
FILL_BITS = 5

_NT = (((1,), (1,)), ((), ()))


def _dot(a, b):
    return jnp.dot(a, b, preferred_element_type=F32)


def _dot_nt(a, b):
    return lax.dot_general(a, b, _NT, preferred_element_type=F32)


def _rms(x):
    return x * lax.rsqrt(jnp.mean(x * x, axis=-1, keepdims=True) + EPS)


def _silu(x):
    return x * jax.nn.sigmoid(x)


def _params(*sem):
    return pltpu.CompilerParams(dimension_semantics=sem, vmem_limit_bytes=VMEM_LIMIT)


def _ada_kernel(c_ref, w_ref, b_ref, o_ref):
    s = _silu(c_ref[...]).astype(BF16)
    o_ref[...] = _dot(s, w_ref[...].astype(BF16)) + b_ref[...]


def _ada(c, w, b):
    n, d = c.shape
    cols = w.shape[1]
    tn = 1536
    return pl.pallas_call(
        _ada_kernel,
        out_shape=jax.ShapeDtypeStruct((n, cols), F32),
        grid=(cols // tn,),
        in_specs=[pl.BlockSpec((n, d), lambda j: (0, 0)),
                  pl.BlockSpec((d, tn), lambda j: (0, j)),
                  pl.BlockSpec((1, tn), lambda j: (0, j))],
        out_specs=pl.BlockSpec((n, tn), lambda j: (0, j)),
        compiler_params=_params("arbitrary"),
        name="ada",
    )(c, w, b.reshape(1, cols))


def _inproj_kernel(x_ref, sc_ref, sh_ref, w_ref, q_ref, ka_ref, va_ref, kb_ref, vb_ref, kv_ref):
    bb, tt, d = x_ref.shape
    h = _rms(x_ref[...]) * (1.0 + sc_ref[...]) + sh_ref[...]
    proj = _dot(h.reshape(bb * tt, d).astype(BF16), w_ref[...])

    def part(lo, width):
        return proj[:, lo:lo + width].reshape(bb, tt, width)

    qa, ka, va = part(0, 512), part(512, 512), part(1024, 512)
    qb, kb, vb = part(1536, 512), part(2048, 512), part(2560, 512)
    q_ref[:, :, 0:512] = (qa * (QK_SCALE * LOG2E)).astype(BF16)
    q_ref[:, :, 512:1024] = (qb * (QK_SCALE * LOG2E)).astype(BF16)
    ka_ref[...] = ka
    va_ref[...] = va
    kb_ref[...] = kb
    vb_ref[...] = vb
    kv_ref[:, :, KV_KA:KV_KA + 512] = ka.astype(BF16)
    kv_ref[:, :, KV_VA:KV_VA + 512] = va.astype(BF16)
    kv_ref[:, :, KV_KB:KV_KB + 512] = kb.astype(BF16)
    kv_ref[:, :, KV_VB:KV_VB + 512] = vb.astype(BF16)


def _inproj(x, sc, sh, w_bf16, bb, tt):
    b, t, d = x.shape
    xspec = pl.BlockSpec((bb, tt, d), lambda i, j: (i, j, 0))
    mspec = pl.BlockSpec((bb, 1, d), lambda i, j: (i, 0, 0))

    def ospec(width):
        return pl.BlockSpec((bb, tt, width), lambda i, j: (i, j, 0))

    f32rows = jax.ShapeDtypeStruct((b, t, 512), F32)
    return pl.pallas_call(
        _inproj_kernel,
        out_shape=(jax.ShapeDtypeStruct((b, t, 1024), BF16), f32rows, f32rows, f32rows, f32rows,
                   jax.ShapeDtypeStruct((b, t, 2048), BF16)),
        grid=(b // bb, t // tt),
        in_specs=[xspec, mspec, mspec, pl.BlockSpec((d, IN_COLS), lambda i, j: (0, 0))],
        out_specs=(ospec(1024), ospec(512), ospec(512), ospec(512), ospec(512), ospec(2048)),
        compiler_params=_params("arbitrary", "arbitrary"),
        name="inproj",
    )(x, sc, sh, w_bf16)


def _lane_lo(shape):
    return lax.broadcasted_iota(jnp.int32, shape, len(shape) - 1) < 64


def _split_halves(x):
    lo = _lane_lo(x.shape)
    zero = jnp.zeros_like(x)
    return jnp.concatenate([jnp.where(lo, x, zero), jnp.where(lo, zero, x)], axis=0)


def _lam_of(lam_ref):
    lp = lam_ref[...]
    s1 = jnp.sum(lp[0:1] * lp[1:2], axis=1, keepdims=True)
    s2 = jnp.sum(lp[2:3] * lp[3:4], axis=1, keepdims=True)
    return jnp.exp(s1) - jnp.exp(s2) + LAM_INIT


def _a_update(state, s, v16):
    m, l, acc = state
    m_new = jnp.maximum(m, jnp.max(s, axis=1, keepdims=True))
    alpha = jnp.exp2(m - m_new)
    p = jnp.exp2(s - m_new)
    l = alpha * l + jnp.sum(p, axis=1, keepdims=True)
    acc = alpha * acc + _dot(p.astype(BF16), v16)
    return m_new, l, acc


def _a_init(rows):
    return (jnp.full((rows, 1), NEG_INF, F32), jnp.zeros((rows, 1), F32), jnp.zeros((rows, LANES), F32))


def _a_finish(state, lam, gain, nq):
    _, l, acc = state
    o = acc[:nq] / l[:nq] - lam * (acc[nq:] / l[nq:])
    return _rms(o) * gain * (1.0 - LAM_INIT)


def _diag_tiles(nq, nk):
    r = lax.broadcasted_iota(jnp.int32, (2 * nq, nk), 0)
    r = jnp.where(r >= nq, r - nq, r)
    c = lax.broadcasted_iota(jnp.int32, (2 * nq, nk), 1)
    visible = (c // CHUNK) <= (r // CHUNK)
    before = c < r
    dist_shift = (r - jnp.abs(r - c)).astype(F32)
    return visible, before, dist_shift


def _strict_upper(n):
    j = lax.broadcasted_iota(jnp.int32, (n, n), 0)
    s = lax.broadcasted_iota(jnp.int32, (n, n), 1)
    return jnp.where(j > s, 1.0, 0.0).astype(BF16)


def _b_block(z, v16, tri, carry, acc, before, nq):
    sp = jnp.where(z > 60.0, z, jnp.log2(1.0 + jnp.exp2(z)))
    if before is not None:
        sp = jnp.where(before, sp, 0.0)
    later = _dot(sp.astype(BF16), tri)
    a = jnp.exp2((z - sp) - (carry + later))
    if before is not None:
        a = jnp.where(before, a, 0.0)
    a16 = a.astype(BF16)
    vlo = _lane_lo(v16.shape)
    vzero = jnp.zeros_like(v16)
    acc = acc + _dot(a16[:nq], jnp.where(vlo, v16, vzero)) + _dot(a16[nq:], jnp.where(vlo, vzero, v16))
    carry = carry + jnp.sum(sp, axis=1, keepdims=True)
    return carry, acc


def _b_finish(acc, gain2):
    lo = _lane_lo(acc.shape)
    ss = acc * acc
    s_lo = jnp.sum(jnp.where(lo, ss, 0.0), axis=1, keepdims=True)
    s_hi = jnp.sum(jnp.where(lo, 0.0, ss), axis=1, keepdims=True)
    ms = jnp.where(lo, s_lo, s_hi) * (1.0 / B_DIM)
    return acc * lax.rsqrt(ms + EPS) * gain2


def _attn_prompt_kernel(lam_ref, ga_ref, gb_ref, q_ref, kv_ref, o_ref):
    bq = q_ref.shape[1]
    pairs = B_HEADS // 2
    i = pl.program_id(1)
    q0 = pl.multiple_of(i * bq, bq)
    lam = _lam_of(lam_ref)
    visible, before, dist_shift = _diag_tiles(bq, bq)
    col = lax.broadcasted_iota(jnp.int32, (1, bq), 1)
    tri = _strict_upper(bq)
    qa = [_split_halves(q_ref[0, :, h * LANES:(h + 1) * LANES]) for h in range(A_HEADS)]
    qb = [_split_halves(q_ref[0, :, A_WIDTH + p * LANES:A_WIDTH + (p + 1) * LANES]) for p in range(pairs)]

    def kv(rows, base, n):
        return kv_ref[0, rows, base + n * LANES:base + (n + 1) * LANES]

    rows = pl.ds(q0, bq)
    a_st = tuple(
        _a_update(_a_init(2 * bq),
                  jnp.where(visible, _dot_nt(qa[h], kv(rows, KV_KA, h)) + SLOPES_LOG2[h] * dist_shift, NEG_INF),
                  kv(rows, KV_VA, h))
        for h in range(A_HEADS))
    b_st = tuple(
        _b_block(_dot_nt(qb[p], kv(rows, KV_KB, p)), kv(rows, KV_VB, p), tri,
                 jnp.zeros((2 * bq, 1), F32), jnp.zeros((bq, LANES), F32), before, bq)
        for p in range(pairs))

    def body(jj, st):
        a_st, b_st = st
        k0 = pl.multiple_of((i - 1 - jj) * bq, bq)
        rows = pl.ds(k0, bq)
        rel = (col + (k0 - q0)).astype(F32)
        a_new = tuple(
            _a_update(a_st[h], _dot_nt(qa[h], kv(rows, KV_KA, h)) + SLOPES_LOG2[h] * rel, kv(rows, KV_VA, h))
            for h in range(A_HEADS))
        b_new = tuple(
            _b_block(_dot_nt(qb[p], kv(rows, KV_KB, p)), kv(rows, KV_VB, p), tri, b_st[p][0], b_st[p][1], None, bq)
            for p in range(pairs))
        return a_new, b_new

    a_st, b_st = lax.fori_loop(0, i, body, (a_st, b_st))
    for h in range(A_HEADS):
        o_ref[0, :, h * LANES:(h + 1) * LANES] = _a_finish(a_st[h], lam, ga_ref[...], bq).astype(o_ref.dtype)
    for p in range(pairs):
        o_ref[0, :, A_WIDTH + p * LANES:A_WIDTH + (p + 1) * LANES] = (
            _b_finish(b_st[p][1], gb_ref[...]).astype(o_ref.dtype))


def _attn_prompt(lam_p, ga, gb2, q, kv, bq):
    b, t, _ = q.shape
    small = lambda shape: pl.BlockSpec(shape, lambda bi, i: (0, 0))
    return pl.pallas_call(
        _attn_prompt_kernel,
        out_shape=jax.ShapeDtypeStruct((b, t, D_MODEL), BF16),
        grid=(b, t // bq),
        in_specs=[small((4, A_QK_DIM)), small((1, LANES)), small((1, LANES)),
                  pl.BlockSpec((1, bq, D_MODEL), lambda bi, i: (bi, i, 0)),
                  pl.BlockSpec((1, t, 2048), lambda bi, i: (bi, 0, 0))],
        out_specs=pl.BlockSpec((1, bq, D_MODEL), lambda bi, i: (bi, i, 0)),
        compiler_params=_params("arbitrary", "arbitrary"),
        name="attn_prompt",
    )(lam_p, ga, gb2, q, kv)


def _attn_sample_kernel(lam_ref, ga_ref, gb_ref, q_ref, kv_ref, cak_ref, cav_ref, cbk_ref, cbv_ref, o_ref):
    nq = q_ref.shape[1]
    past = cak_ref.shape[1]
    pb = 256
    lam = _lam_of(lam_ref)
    visible, before, dist_shift = _diag_tiles(nq, nq)
    pcol = lax.broadcasted_iota(jnp.int32, (1, past), 1)
    tri_new = _strict_upper(nq)
    tri_past = _strict_upper(pb)

    for h in range(A_HEADS):
        cs = slice(h * LANES, (h + 1) * LANES)
        qs = _split_halves(q_ref[0, :, cs])
        slope = SLOPES_LOG2[h]
        s_past = _dot_nt(qs, cak_ref[0, :, cs].astype(BF16)) + slope * (pcol - past).astype(F32)
        state = _a_update(_a_init(2 * nq), s_past, cav_ref[0, :, cs].astype(BF16))
        k16 = kv_ref[0, :, KV_KA + h * LANES:KV_KA + (h + 1) * LANES]
        v16 = kv_ref[0, :, KV_VA + h * LANES:KV_VA + (h + 1) * LANES]
        s_new = jnp.where(visible, _dot_nt(qs, k16) + slope * dist_shift, NEG_INF)
        state = _a_update(state, s_new, v16)
        o_ref[0, :, cs] = _a_finish(state, lam, ga_ref[...], nq).astype(o_ref.dtype)

    for p in range(B_HEADS // 2):
        cs = slice(p * LANES, (p + 1) * LANES)
        qs = _split_halves(q_ref[0, :, A_WIDTH + p * LANES:A_WIDTH + (p + 1) * LANES])
        k16 = kv_ref[0, :, KV_KB + p * LANES:KV_KB + (p + 1) * LANES]
        v16 = kv_ref[0, :, KV_VB + p * LANES:KV_VB + (p + 1) * LANES]
        carry, acc = _b_block(_dot_nt(qs, k16), v16, tri_new,
                              jnp.zeros((2 * nq, 1), F32), jnp.zeros((nq, LANES), F32), before, nq)
        for j in reversed(range(past // pb)):
            rows = slice(j * pb, (j + 1) * pb)
            z = _dot_nt(qs, cbk_ref[0, rows, cs].astype(BF16))
            carry, acc = _b_block(z, cbv_ref[0, rows, cs].astype(BF16), tri_past, carry, acc, None, nq)
        o_ref[0, :, A_WIDTH + p * LANES:A_WIDTH + (p + 1) * LANES] = _b_finish(acc, gb_ref[...]).astype(o_ref.dtype)


def _attn_sample(lam_p, ga, gb2, q, kv, cak, cav, cbk, cbv):
    b, t, _ = q.shape
    past = cak.shape[1]
    small = lambda shape: pl.BlockSpec(shape, lambda bi: (0, 0))
    per_b = lambda rows, width: pl.BlockSpec((1, rows, width), lambda bi: (bi, 0, 0))
    return pl.pallas_call(
        _attn_sample_kernel,
        out_shape=jax.ShapeDtypeStruct((b, t, D_MODEL), BF16),
        grid=(b,),
        in_specs=[small((4, A_QK_DIM)), small((1, LANES)), small((1, LANES)),
                  per_b(t, D_MODEL), per_b(t, 2048),
                  per_b(past, 512), per_b(past, 512), per_b(past, 512), per_b(past, 512)],
        out_specs=per_b(t, D_MODEL),
        compiler_params=_params("arbitrary"),
        name="attn_sample",
    )(lam_p, ga, gb2, q, kv, cak, cav, cbk, cbv)


def _route(logits_t, bias_col):
    n = logits_t.shape[1]
    scores = jax.nn.sigmoid(logits_t)
    sel3 = (scores + bias_col).reshape(N_GROUPS, 8, n)
    eio = lax.broadcasted_iota(jnp.int32, (N_GROUPS, 8, n), 1)
    m1 = jnp.max(sel3, axis=1, keepdims=True)
    i1 = jnp.min(jnp.where(sel3 == m1, eio, 8), axis=1, keepdims=True)
    m2 = jnp.max(jnp.where(eio == i1, -jnp.inf, sel3), axis=1, keepdims=True)
    gscore = (m1 + m2).reshape(N_GROUPS, n)
    gio = lax.broadcasted_iota(jnp.int32, (N_GROUPS, n), 0)
    keep = jnp.zeros((N_GROUPS, n), F32)
    for _ in range(TOPK_GROUPS):
        g = jnp.max(gscore, axis=0, keepdims=True)
        gi = jnp.min(jnp.where(gscore == g, gio, N_GROUPS), axis=0, keepdims=True)
        hit = gio == gi
        keep = jnp.where(hit, 1.0, keep)
        gscore = jnp.where(hit, -jnp.inf, gscore)
    cur = jnp.where(keep.reshape(N_GROUPS, 1, n) > 0.0, sel3, NEG_INF).reshape(N_EXPERTS, n)
    xio = lax.broadcasted_iota(jnp.int32, (N_EXPERTS, n), 0)
    chosen = jnp.zeros((N_EXPERTS, n), F32)
    ids, raw = [], []
    for _ in range(TOP_K):
        mx = jnp.max(cur, axis=0, keepdims=True)
        ei = jnp.min(jnp.where(cur == mx, xio, N_EXPERTS), axis=0, keepdims=True)
        hit = xio == ei
        chosen = jnp.where(hit, 1.0, chosen)
        cur = jnp.where(hit, -jnp.inf, cur)
        ids.append(ei.astype(F32))
        raw.append(jnp.sum(jnp.where(hit, scores, 0.0), axis=0, keepdims=True))
    denom = raw[0]
    for r in raw[1:]:
        denom = denom + r
    wts = [r / denom * ROUTED_SCALE for r in raw]
    pad = jnp.zeros((8 - TOP_K, n), F32)
    return jnp.concatenate(ids + [pad] + wts + [pad], axis=0), chosen


def _outproj_kernel(o_ref, x_ref, g1_ref, sc_ref, sh_ref, wout_ref, wrt_ref, br_ref, x1_ref, h2_ref, rt_ref, cnt_ref):
    bb, tt, d = x_ref.shape
    n = bb * tt
    proj = _dot(o_ref[...].reshape(n, d), wout_ref[...]).reshape(bb, tt, d)
    x1 = x_ref[...] + g1_ref[...] * proj
    x1_ref[...] = x1
    h2 = (_rms(x1) * (1.0 + sc_ref[...]) + sh_ref[...]).reshape(n, d)
    h2hi = h2.astype(BF16)
    h2_ref[...] = h2hi.reshape(bb, tt, d)
    h2lo = (h2 - h2hi.astype(F32)).astype(BF16)
    w = wrt_ref[...]
    whi = w.astype(BF16)
    wlo = (w - whi.astype(F32)).astype(BF16)
    logits_t = _dot_nt(whi, h2hi) + _dot_nt(whi, h2lo) + _dot_nt(wlo, h2hi)
    rt, chosen = _route(logits_t, br_ref[...])
    rt_ref[...] = rt
    ones = jnp.ones((8, TM), BF16)
    for j in range(n // TM):
        cnt_ref[j] = _dot_nt(ones, chosen[:, j * TM:(j + 1) * TM].astype(BF16))


def _outproj(o, x, g1, sc, sh, wout_bf16, wr_t, br_col, bb, tt):
    b, t, d = x.shape
    n = bb * tt
    steps_t = t // tt
    tok = pl.BlockSpec((bb, tt, d), lambda i, j: (i, j, 0))
    mspec = pl.BlockSpec((bb, 1, d), lambda i, j: (i, 0, 0))
    full = lambda shape: pl.BlockSpec(shape, lambda i, j: (0, 0))
    return pl.pallas_call(
        _outproj_kernel,
        out_shape=(jax.ShapeDtypeStruct((b, t, d), F32), jax.ShapeDtypeStruct((b, t, d), BF16),
                   jax.ShapeDtypeStruct((16, b * t), F32),
                   jax.ShapeDtypeStruct((b * t // TM, 8, N_EXPERTS), F32)),
        grid=(b // bb, steps_t),
        in_specs=[tok, tok, mspec, mspec, mspec, full((d, d)), full((N_EXPERTS, d)), full((N_EXPERTS, 1))],
        out_specs=(tok, tok,
                   pl.BlockSpec((16, n), lambda i, j: (0, i * steps_t + j)),
                   pl.BlockSpec((n // TM, 8, N_EXPERTS), lambda i, j: (i * steps_t + j, 0, 0))),
        compiler_params=_params("arbitrary", "arbitrary"),
        name="outproj_router",
    )(o, x, g1, sc, sh, wout_bf16, wr_t, br_col)


PACK_SHIFT = 1
PACK = 1 << PACK_SHIFT
GRAN = SEG // PACK
U32 = jnp.uint32


def _seg_chunks(seg_ref, np_ref, base_ref, g, cap, visit):
    def body(e, c):
        idx = g * N_EXPERTS + e
        s0 = seg_ref[idx] >> PACK_SHIFT
        n16 = np_ref[idx]
        b0 = (e * cap + base_ref[idx]) >> PACK_SHIFT
        for b in range(SEG_BITS):
            off = (n16 & ((1 << b) - 1)) * GRAN

            @pl.when(((n16 >> b) & 1) == 1)
            def _(b=b, off=off):
                visit(pl.multiple_of(s0 + off, GRAN), pl.multiple_of(b0 + off, GRAN), GRAN << b)
        return c
    lax.fori_loop(0, N_EXPERTS, body, 0)


def _tile_rows(seg_ref, np_ref, g):
    last = g * N_EXPERTS + N_EXPERTS - 1
    return seg_ref[last] + np_ref[last] * SEG


def _wait_tile(seg_ref, np_ref, g, wait_rows):
    n16 = _tile_rows(seg_ref, np_ref, g) >> SEG_SHIFT
    for b in range(TILE_BITS):
        @pl.when(((n16 >> b) & 1) == 1)
        def _(b=b):
            wait_rows(GRAN << b)


def _dispatch_kernel(seg_ref, np_ref, base_ref, fin_ref, rem_ref, rt_ref, h2_ref, *rest, t0, nsteps, cap, fill):
    xs_ref, pos_ref, buf, zbuf, sem, zsem = rest[-6:]
    s = pl.program_id(0)
    g = t0 + s
    slot = s % 2

    def copy(slot_, row, hrow, size):
        return pltpu.make_async_copy(buf.at[slot_, pl.ds(row, size)], xs_ref.at[pl.ds(hrow, size)], sem.at[slot_])

    def chunks(g_, fn):
        _seg_chunks(seg_ref, np_ref, base_ref, g_, cap, fn)

    def wait_tile(g_, slot_):
        _wait_tile(seg_ref, np_ref, g_, lambda n: copy(slot_, 0, 0, n).wait())

    @pl.when(s >= 2)
    def _():
        wait_tile(g - 2, slot)

    rt = rt_ref[...]
    eio = lax.broadcasted_iota(jnp.int32, (N_EXPERTS, TM), 0).astype(F32)
    chosen = jnp.zeros((N_EXPERTS, TM), F32)
    for k in range(TOP_K):
        chosen = jnp.where(eio == rt[k:k + 1], 1.0, chosen)
    t_row = lax.broadcasted_iota(jnp.int32, (TM, TM), 0)
    t_col = lax.broadcasted_iota(jnp.int32, (TM, TM), 1)
    rank = _dot(chosen.astype(BF16), jnp.where(t_row < t_col, 1.0, 0.0).astype(BF16))
    n_e = jnp.sum(chosen, axis=1, keepdims=True).astype(jnp.int32)
    np16 = ((n_e + (SEG - 1)) >> SEG_SHIFT).astype(F32)
    e_row = lax.broadcasted_iota(jnp.int32, (N_EXPERTS, N_EXPERTS), 0)
    e_col = lax.broadcasted_iota(jnp.int32, (N_EXPERTS, N_EXPERTS), 1)
    lower = jnp.where(e_col < e_row, 1.0, 0.0).astype(BF16)
    seg0 = SEG * _dot(lower, jnp.broadcast_to(np16, (N_EXPERTS, LANES)).astype(BF16))[:, 0:1]
    pos = seg0 + rank
    pos_rows = [jnp.sum(jnp.where(eio == rt[k:k + 1], pos, 0.0), axis=0, keepdims=True) for k in range(TOP_K)]
    pos_ref[...] = jnp.concatenate(pos_rows + [jnp.zeros((8 - TOP_K, TM), F32)], axis=0)

    h2 = h2_ref[...]
    npairs = (_tile_rows(seg_ref, np_ref, g) + (2 * CH - 1)) >> (CH_SHIFT + 1)

    def chunk_pair(c2, carry):
        for half in range(2):
            r0 = pl.multiple_of(c2 * (2 * CH) + half * CH, CH)
            rio = (lax.broadcasted_iota(jnp.int32, (CH, TM), 0) + r0).astype(F32)
            p = jnp.zeros((CH, TM), F32)
            for k in range(TOP_K):
                p = jnp.where(rio == pos_rows[k], 1.0, p)
            rows16 = _dot(p.astype(BF16), h2).astype(BF16)
            buf[slot, pl.ds(pl.multiple_of(r0 >> PACK_SHIFT, CH // PACK), CH // PACK), :] = pltpu.bitcast(rows16, U32)
        return carry
    lax.fori_loop(0, npairs, chunk_pair, 0)

    chunks(g, lambda r, hr, n: copy(slot, r, hr, n).start())

    @pl.when(s == nsteps - 1)
    def _():
        if nsteps >= 2:
            wait_tile(g - 1, 1 - slot)
        wait_tile(g, slot)
        if fill:
            zbuf[...] = jnp.zeros_like(zbuf)

            def zero_chunks(fn):
                def body(e, c):
                    r16 = rem_ref[e]
                    h0 = (e * cap + fin_ref[e]) >> PACK_SHIFT
                    for b in range(FILL_BITS):
                        off = (r16 & ((1 << b) - 1)) * GRAN

                        @pl.when(((r16 >> b) & 1) == 1)
                        def _(b=b, off=off):
                            size = GRAN << b
                            fn(pltpu.make_async_copy(zbuf.at[pl.ds(0, size)],
                                                     xs_ref.at[pl.ds(pl.multiple_of(h0 + off, GRAN), size)], zsem))
                    return c
                lax.fori_loop(0, N_EXPERTS, body, 0)
            zero_chunks(lambda cp: cp.start())
            zero_chunks(lambda cp: cp.wait())


def _dispatch(plan, rt, h2, xs, t0, cap, fill):
    n, d = h2.shape
    nsteps = n // TM
    kern = functools.partial(_dispatch_kernel, t0=t0, nsteps=nsteps, cap=cap, fill=fill)
    in_specs = [pl.BlockSpec((16, TM), lambda s, *_: (0, s)), pl.BlockSpec((TM, d), lambda s, *_: (s, 0))]
    args = [rt, h2]
    aliases = {}
    if xs is not None:
        in_specs.append(pl.BlockSpec(memory_space=pl.ANY))
        args.append(xs)
        aliases = {len(plan) + 2: 0}
    return pl.pallas_call(
        kern,
        out_shape=(jax.ShapeDtypeStruct((N_EXPERTS * cap // PACK, d), U32), jax.ShapeDtypeStruct((8, n), F32)),
        grid_spec=pltpu.PrefetchScalarGridSpec(
            num_scalar_prefetch=len(plan), grid=(nsteps,), in_specs=in_specs,
            out_specs=(pl.BlockSpec(memory_space=pl.ANY), pl.BlockSpec((8, TM), lambda s, *_: (0, s))),
            scratch_shapes=[pltpu.VMEM((2, RMAX // PACK, d), U32), pltpu.VMEM((TR // 2 // PACK, d), U32),
                            pltpu.SemaphoreType.DMA((2,)), pltpu.SemaphoreType.DMA(())]),
        input_output_aliases=aliases,
        compiler_params=_params("arbitrary"),
        name="dispatch",
    )(*plan, *args)


def _expert_kernel(te_ref, tb_ref, tv_ref, tf_ref, xs_ref, wg_ref, wu_ref, wd_ref, ys_ref, wg16, wu16, wd16):
    i = pl.program_id(0)

    @pl.when(tf_ref[i] == 1)
    def _():
        wg16[...] = wg_ref[0].astype(BF16)
        wu16[...] = wu_ref[0].astype(BF16)
        wd16[...] = wd_ref[0].astype(BF16)

    @pl.when(tv_ref[i] == 1)
    def _():
        x = pltpu.bitcast(xs_ref[...], BF16)
        act = _silu(_dot(x, wg16[...])) * _dot(x, wu16[...])
        ys_ref[...] = pltpu.bitcast(_dot(act.astype(BF16), wd16[...]).astype(BF16), U32)


def _experts(sched, xs, wg, wu, wd, max_tiles):
    d = xs.shape[1]
    wspec = lambda shape: pl.BlockSpec((1,) + shape, lambda i, te, tb, tv, tf: (te[i], 0, 0))
    row_map = lambda i, te, tb, tv, tf: (tb[i], 0)
    return pl.pallas_call(
        _expert_kernel,
        out_shape=jax.ShapeDtypeStruct(xs.shape, U32),
        grid_spec=pltpu.PrefetchScalarGridSpec(
            num_scalar_prefetch=4, grid=(max_tiles,),
            in_specs=[pl.BlockSpec((TR // PACK, d), row_map),
                      wspec((d, D_EXPERT)), wspec((d, D_EXPERT)), wspec((D_EXPERT, d))],
            out_specs=pl.BlockSpec((TR // PACK, d), row_map),
            scratch_shapes=[pltpu.VMEM((d, D_EXPERT), BF16), pltpu.VMEM((d, D_EXPERT), BF16),
                            pltpu.VMEM((D_EXPERT, d), BF16)]),
        compiler_params=_params("arbitrary"),
        name="experts",
    )(*sched, xs, wg, wu, wd)


def _combine_kernel(seg_ref, np_ref, base_ref, pos_ref, rt_ref, x1_ref, h2_ref, g2_ref, wsg_ref, wsu_ref, wsd_ref,
                    gf_ref, ys_ref, y_ref, ybuf, acc, sem, *, t0, nsteps, cap):
    bb, tt, d = x1_ref.shape
    s = pl.program_id(0)
    g = t0 + s
    slot = s % 2

    def copy(slot_, row, hrow, size):
        return pltpu.make_async_copy(ys_ref.at[pl.ds(hrow, size)], ybuf.at[slot_, pl.ds(row, size)], sem.at[slot_])

    def chunks(g_, fn):
        _seg_chunks(seg_ref, np_ref, base_ref, g_, cap, fn)

    @pl.when(s == 0)
    def _():
        ybuf[...] = jnp.zeros_like(ybuf)
        chunks(g, lambda r, hr, n: copy(0, r, hr, n).start())

    @pl.when(s + 1 < nsteps)
    def _():
        chunks(g + 1, lambda r, hr, n: copy(1 - slot, r, hr, n).start())

    _wait_tile(seg_ref, np_ref, g, lambda n: copy(slot, 0, 0, n).wait())

    stacked = jnp.concatenate([pos_ref[...], jnp.zeros((8, TM), F32), rt_ref[...],
                               jnp.zeros((LANES - 32, TM), F32)], axis=0)
    cols = stacked.T
    pos_b = [jnp.broadcast_to(cols[:, k:k + 1], (TM, LANES)) for k in range(TOP_K)]
    wt_b = [jnp.broadcast_to(cols[:, 24 + k:25 + k], (TM, LANES)) for k in range(TOP_K)]
    npairs = (_tile_rows(seg_ref, np_ref, g) + (2 * CH - 1)) >> (CH_SHIFT + 1)
    acc[...] = jnp.zeros_like(acc)

    def chunk_pair(c2, carry):
        part = None
        for half in range(2):
            r0 = pl.multiple_of(c2 * (2 * CH) + half * CH, CH)
            w_parts = []
            for lane0 in range(0, CH, LANES):
                rio = (lax.broadcasted_iota(jnp.int32, (TM, LANES), 1) + (r0 + lane0)).astype(F32)
                w = jnp.zeros((TM, LANES), F32)
                for k in range(TOP_K):
                    w = jnp.where(rio == pos_b[k], wt_b[k], w)
                w_parts.append(w.astype(BF16))
            rows = ybuf[slot, pl.ds(pl.multiple_of(r0 >> PACK_SHIFT, CH // PACK), CH // PACK), :]
            prod = _dot(jnp.concatenate(w_parts, axis=1), pltpu.bitcast(rows, BF16))
            part = prod if part is None else part + prod
        acc[...] += part
        return carry
    lax.fori_loop(0, npairs, chunk_pair, 0)

    h = h2_ref[...].reshape(bb * tt, d)
    act = _silu(_dot(h, wsg_ref[...])) * _dot(h, wsu_ref[...])
    moe = acc[...] + _dot(act.astype(BF16), wsd_ref[...])
    x2 = x1_ref[...] + g2_ref[...] * moe.reshape(bb, tt, d)
    y_ref[...] = _rms(x2) * gf_ref[...]


def _combine(plan3, pos, rt, x1, h2, g2, wsg, wsu, wsd, gf, ys, bb, tt, t0, cap):
    b, t, d = x1.shape
    steps_t = t // tt
    nsteps = (b // bb) * steps_t
    bidx = lambda s: (s // steps_t, s % steps_t, 0)
    tok = pl.BlockSpec((bb, tt, d), lambda s, *_: bidx(s))
    full = lambda shape: pl.BlockSpec(shape, lambda s, *_: (0,) * len(shape))
    kern = functools.partial(_combine_kernel, t0=t0, nsteps=nsteps, cap=cap)
    return pl.pallas_call(
        kern,
        out_shape=jax.ShapeDtypeStruct((b, t, d), F32),
        grid_spec=pltpu.PrefetchScalarGridSpec(
            num_scalar_prefetch=3, grid=(nsteps,),
            in_specs=[pl.BlockSpec((8, TM), lambda s, *_: (0, s)), pl.BlockSpec((16, TM), lambda s, *_: (0, s)),
                      tok, tok, pl.BlockSpec((bb, 1, d), lambda s, *_: (s // steps_t, 0, 0)),
                      full(wsg.shape), full(wsu.shape), full(wsd.shape), full((1, 1, d)),
                      pl.BlockSpec(memory_space=pl.ANY)],
            out_specs=tok,
            scratch_shapes=[pltpu.VMEM((2, RMAX // PACK, d), U32), pltpu.VMEM((TM, d), F32),
                            pltpu.SemaphoreType.DMA((2,))]),
        compiler_params=_params("arbitrary"),
        name="combine",
    )(*plan3, pos, rt, x1, h2, g2, wsg, wsu, wsd, gf, ys)


def _plan(cnt, cap, max_tiles):
    np16 = (cnt + (SEG - 1)) // SEG
    npad = np16 * SEG
    seg = jnp.cumsum(npad, axis=1) - npad
    base = jnp.cumsum(npad, axis=0) - npad
    fin = jnp.sum(npad, axis=0)
    rem16 = ((-fin) % TR) // SEG
    tiles_e = (fin + (TR - 1)) // TR
    ends = jnp.cumsum(tiles_e)
    i = jnp.arange(max_tiles, dtype=jnp.int32)
    valid = i < ends[-1]
    ii = jnp.where(valid, i, jnp.maximum(ends[-1] - 1, 0))
    te = jnp.sum((ends[None, :] <= ii[:, None]).astype(jnp.int32), axis=1)
    te = jnp.minimum(te, N_EXPERTS - 1)
    r = ii - (ends - tiles_e)[te]
    tb = te * (cap // TR) + r
    first = jnp.logical_and(valid, r == 0)
    i32 = lambda a: a.astype(jnp.int32).reshape(-1)
    return (i32(seg), i32(np16), i32(base), i32(fin), i32(rem16)), (te, i32(tb), i32(valid), i32(first))


def kernel(x_prompt, x_sample, cache_a_k, cache_a_v, cache_b_k, cache_b_v, c_prompt, c_sample, w_ada, b_ada, w_in, lam_params, g_sub_a, g_sub_b, w_out, w_router, b_router, w_gate, w_up, w_down, w_shared_gate, w_shared_up, w_shared_down, g_final):
    bp, tp, d = x_prompt.shape
    bs, ts, _ = x_sample.shape
    past = cache_a_k.shape[2]

    mod = _ada(jnp.concatenate([c_prompt, c_sample], axis=0), w_ada[0], b_ada[0])
    mod = mod.reshape(bp + bs, 6, 1, d)
    sh1, sc1, g1, sh2, sc2, g2 = (mod[:, k] for k in range(6))

    w_in16 = w_in[0].astype(BF16)
    w_out16 = w_out[0].astype(BF16)
    wr_t = w_router[0].T
    br_col = b_router[0].reshape(N_EXPERTS, 1)
    wsg16, wsu16, wsd16 = (w_shared_gate[0].astype(BF16), w_shared_up[0].astype(BF16),
                           w_shared_down[0].astype(BF16))
    ga = g_sub_a[0].reshape(1, LANES)
    gb2 = jnp.concatenate([g_sub_b[0], g_sub_b[0]]).reshape(1, LANES)
    gf = g_final.reshape(1, 1, d)
    lam_p = lam_params[0]

    def front(x, sl, bb, tt, attn):
        q, ka, va, kb, vb, kv = _inproj(x, sc1[sl], sh1[sl], w_in16, bb, tt)
        o = attn(q, kv)
        x1, h2, rt, cnt = _outproj(o, x, g1[sl], sc2[sl], sh2[sl], w_out16, wr_t, br_col, bb, tt)
        return (x1, h2, rt, cnt), (ka, va, kb, vb)

    mid_p, rows_p = front(x_prompt, slice(0, bp), 1, 512,
                          lambda q, kv: _attn_prompt(lam_p, ga, gb2, q, kv, 256))
    caches = (cache_a_k[0].reshape(bs, past, 512), cache_a_v[0].reshape(bs, past, 512),
              cache_b_k[0].reshape(bs, past, 512), cache_b_v[0].reshape(bs, past, 512))
    mid_s, rows_s = front(x_sample, slice(bp, bp + bs), 8, ts,
                          lambda q, kv: _attn_sample(lam_p, ga, gb2, q, kv, *caches))

    (x1_p, h2_p, rt_p, cnt_p), (x1_s, h2_s, rt_s, cnt_s) = mid_p, mid_s
    n_p, n_s = bp * tp, bs * ts
    tiles_p, tiles_s = n_p // TM, n_s // TM
    tiles = tiles_p + tiles_s
    cap = -(-(n_p + n_s + tiles * (SEG - 1)) // TR) * TR
    max_tiles = (TOP_K * (n_p + n_s) + tiles * N_EXPERTS * (SEG - 1)) // TR + N_EXPERTS
    cnt = jnp.concatenate([cnt_p[:, 0, :], cnt_s[:, 0, :]], axis=0).astype(jnp.int32)
    plan, sched = _plan(cnt, cap, max_tiles)
    xs, pos_p = _dispatch(plan, rt_p, h2_p.reshape(n_p, d), None, 0, cap, False)
    xs, pos_s = _dispatch(plan, rt_s, h2_s.reshape(n_s, d), xs, tiles_p, cap, True)
    ys = _experts(sched, xs, w_gate[0], w_up[0], w_down[0], max_tiles)
    y_p = _combine(plan[:3], pos_p, rt_p, x1_p, h2_p, g2[0:bp], wsg16, wsu16, wsd16, gf, ys,
                   1, TM, 0, cap)
    y_s = _combine(plan[:3], pos_s, rt_s, x1_s, h2_s, g2[bp:bp + bs], wsg16, wsu16, wsd16, gf, ys,
                   TM // ts, ts, tiles_p, cap)

    def shape_rows(rows, b, t):
        ka, va, kb, vb = rows
        return (ka.reshape(1, b, t, A_HEADS, 2 * A_QK_DIM), va.reshape(1, b, t, A_HEADS, 2 * A_QK_DIM),
                kb.reshape(1, b, t, B_HEADS, B_DIM), vb.reshape(1, b, t, B_HEADS, B_DIM))

    return (y_p, y_s) + shape_rows(rows_p, bp, tp) + shape_rows(rows_s, bs, ts)
```

```python
import functools
import math

import jax
import jax.numpy as jnp
from jax import lax
from jax.experimental import pallas as pl
from jax.experimental.pallas import tpu as pltpu

F32 = jnp.float32
BF16 = jnp.bfloat16

D_MODEL = 1024
CHUNK = 64
A_QK_DIM = 64
A_HEADS = 4
A_WIDTH = 512
B_DIM = 64
B_HEADS = 8
B_WIDTH = 512
IN_COLS = 3 * A_WIDTH + 3 * B_WIDTH
N_EXPERTS = 64
TOP_K = 6
N_GROUPS = 8
TOPK_GROUPS = 4
D_EXPERT = 256
ROUTED_SCALE = 2.5
EPS = 1e-6
NEG_INF = -1e30
LANES = 128
QK_SCALE = A_QK_DIM ** -0.5
LAM_INIT = 0.8 - 0.6 * math.exp(-0.3 * 0)
LOG2E = math.log2(math.e)
SLOPES_LOG2 = tuple(LOG2E * 2.0 ** (-8.0 * (h + 1.0) / A_HEADS) for h in range(A_HEADS))
KV_KA, KV_VA, KV_KB, KV_VB = 0, 512, 1024, 1536
VMEM_LIMIT = 56 * 1024 * 1024
TM = 256
SEG_SHIFT = 4
SEG = 1 << SEG_SHIFT
SEG_BITS = 5
CH_SHIFT = 8
CH = 1 << CH_SHIFT
RMAX = 2560
TILE_BITS = 8
TR = 1024
FILL_BITS = 6

_NT = (((1,), (1,)), ((), ()))


def _dot(a, b):
    return jnp.dot(a, b, preferred_element_type=F32)


def _dot_nt(a, b):
    return lax.dot_general(a, b, _NT, preferred_element_type=F32)


def _rms(x):
    return x * lax.rsqrt(jnp.mean(x * x, axis=-1, keepdims=True) + EPS)


def _silu(x):
    return x * jax.nn.sigmoid(x)


def _params(*sem):
    return pltpu.CompilerParams(dimension_semantics=sem, vmem_limit_bytes=VMEM_LIMIT)


def _ada_kernel(c_ref, w_ref, b_ref, o_ref):
    s = _silu(c_ref[...]).astype(BF16)
    o_ref[...] = _dot(s, w_ref[...].astype(BF16)) + b_ref[...]


def _ada(c, w, b):
    n, d = c.shape
    cols = w.shape[1]
    tn = 1536
    return pl.pallas_call(
        _ada_kernel,
        out_shape=jax.ShapeDtypeStruct((n, cols), F32),
        grid=(cols // tn,),
        in_specs=[pl.BlockSpec((n, d), lambda j: (0, 0)),
                  pl.BlockSpec((d, tn), lambda j: (0, j)),
                  pl.BlockSpec((1, tn), lambda j: (0, j))],
        out_specs=pl.BlockSpec((n, tn), lambda j: (0, j)),
        compiler_params=_params("arbitrary"),
        name="ada",
    )(c, w, b.reshape(1, cols))


def _inproj_kernel(x_ref, sc_ref, sh_ref, w_ref, q_ref, ka_ref, va_ref, kb_ref, vb_ref, kv_ref):
    bb, tt, d = x_ref.shape
    h = _rms(x_ref[...]) * (1.0 + sc_ref[...]) + sh_ref[...]
    proj = _dot(h.reshape(bb * tt, d).astype(BF16), w_ref[...])

    def part(lo, width):
        return proj[:, lo:lo + width].reshape(bb, tt, width)

    qa, ka, va = part(0, 512), part(512, 512), part(1024, 512)
    qb, kb, vb = part(1536, 512), part(2048, 512), part(2560, 512)
    q_ref[:, :, 0:512] = (qa * (QK_SCALE * LOG2E)).astype(BF16)
    q_ref[:, :, 512:1024] = (qb * (QK_SCALE * LOG2E)).astype(BF16)
    for hd in range(A_HEADS):
        ka_ref[:, pl.ds(hd, tt, stride=A_HEADS), :] = ka[:, :, hd * LANES:(hd + 1) * LANES]
        va_ref[:, pl.ds(hd, tt, stride=A_HEADS), :] = va[:, :, hd * LANES:(hd + 1) * LANES]
    kb_ref[...] = kb
    vb_ref[...] = vb
    kv_ref[:, :, KV_KA:KV_KA + 512] = ka.astype(BF16)
    kv_ref[:, :, KV_VA:KV_VA + 512] = va.astype(BF16)
    kv_ref[:, :, KV_KB:KV_KB + 512] = kb.astype(BF16)
    kv_ref[:, :, KV_VB:KV_VB + 512] = vb.astype(BF16)


def _inproj(x, sc, sh, w_bf16, bb, tt):
    b, t, d = x.shape
    xspec = pl.BlockSpec((bb, tt, d), lambda i, j: (i, j, 0))
    mspec = pl.BlockSpec((bb, 1, d), lambda i, j: (i, 0, 0))

    def ospec(width):
        return pl.BlockSpec((bb, tt, width), lambda i, j: (i, j, 0))

    f32rows = jax.ShapeDtypeStruct((b, t, 512), F32)
    a_rows = jax.ShapeDtypeStruct((b, t * A_HEADS, LANES), F32)
    a_spec = pl.BlockSpec((bb, tt * A_HEADS, LANES), lambda i, j: (i, j, 0))
    return pl.pallas_call(
        _inproj_kernel,
        out_shape=(jax.ShapeDtypeStruct((b, t, 1024), BF16), a_rows, a_rows, f32rows, f32rows,
                   jax.ShapeDtypeStruct((b, t, 2048), BF16)),
        grid=(b // bb, t // tt),
        in_specs=[xspec, mspec, mspec, pl.BlockSpec((d, IN_COLS), lambda i, j: (0, 0))],
        out_specs=(ospec(1024), a_spec, a_spec, ospec(512), ospec(512), ospec(2048)),
        compiler_params=_params("arbitrary", "arbitrary"),
        name="inproj",
    )(x, sc, sh, w_bf16)


def _lane_lo(shape):
    return lax.broadcasted_iota(jnp.int32, shape, len(shape) - 1) < 64


def _split_halves(x):
    lo = _lane_lo(x.shape)
    zero = jnp.zeros_like(x)
    return jnp.concatenate([jnp.where(lo, x, zero), jnp.where(lo, zero, x)], axis=0)


def _lam_of(lam_ref):
    lp = lam_ref[...]
    s1 = jnp.sum(lp[0:1] * lp[1:2], axis=1, keepdims=True)
    s2 = jnp.sum(lp[2:3] * lp[3:4], axis=1, keepdims=True)
    return jnp.exp(s1) - jnp.exp(s2) + LAM_INIT


def _a_update(state, s, v16):
    m, l, acc = state
    m_new = jnp.maximum(m, jnp.max(s, axis=1, keepdims=True))
    alpha = jnp.exp2(m - m_new)
    p = jnp.exp2(s - m_new)
    l = alpha * l + jnp.sum(p, axis=1, keepdims=True)
    acc = alpha * acc + _dot(p.astype(BF16), v16)
    return m_new, l, acc


def _a_init(rows):
    return (jnp.full((rows, 1), NEG_INF, F32), jnp.zeros((rows, 1), F32), jnp.zeros((rows, LANES), F32))


def _a_finish(state, lam, gain, nq):
    _, l, acc = state
    o = acc[:nq] / l[:nq] - lam * (acc[nq:] / l[nq:])
    return _rms(o) * gain * (1.0 - LAM_INIT)


def _diag_tiles(nq, nk):
    r = lax.broadcasted_iota(jnp.int32, (2 * nq, nk), 0)
    r = jnp.where(r >= nq, r - nq, r)
    c = lax.broadcasted_iota(jnp.int32, (2 * nq, nk), 1)
    visible = (c // CHUNK) <= (r // CHUNK)
    before = c < r
    dist_shift = (r - jnp.abs(r - c)).astype(F32)
    return visible, before, dist_shift


def _strict_upper(n):
    j = lax.broadcasted_iota(jnp.int32, (n, n), 0)
    s = lax.broadcasted_iota(jnp.int32, (n, n), 1)
    return jnp.where(j > s, 1.0, 0.0).astype(BF16)


def _b_block(z, v16, tri, carry, acc, before, nq, v_transposed=False):
    sp = jnp.where(z > 60.0, z, jnp.log2(1.0 + jnp.exp2(z)))
    if before is not None:
        sp = jnp.where(before, sp, 0.0)
    later = _dot(sp.astype(BF16), tri)
    a = jnp.exp2((z - sp) - (carry + later))
    if before is not None:
        a = jnp.where(before, a, 0.0)
    a16 = a.astype(BF16)
    vzero = jnp.zeros_like(v16)
    if v_transposed:
        first = lax.broadcasted_iota(jnp.int32, v16.shape, 0) < B_DIM
        acc = acc + _dot_nt(a16[:nq], jnp.where(first, v16, vzero)) + _dot_nt(a16[nq:], jnp.where(first, vzero, v16))
    else:
        vlo = _lane_lo(v16.shape)
        acc = acc + _dot(a16[:nq], jnp.where(vlo, v16, vzero)) + _dot(a16[nq:], jnp.where(vlo, vzero, v16))
    carry = carry + jnp.sum(sp, axis=1, keepdims=True)
    return carry, acc


def _b_finish(acc, gain2):
    lo = _lane_lo(acc.shape)
    ss = acc * acc
    s_lo = jnp.sum(jnp.where(lo, ss, 0.0), axis=1, keepdims=True)
    s_hi = jnp.sum(jnp.where(lo, 0.0, ss), axis=1, keepdims=True)
    ms = jnp.where(lo, s_lo, s_hi) * (1.0 / B_DIM)
    return acc * lax.rsqrt(ms + EPS) * gain2


def _attn_prompt_kernel(lam_ref, ga_ref, gb_ref, q_ref, kv_ref, o_ref):
    bq = q_ref.shape[1]
    pairs = B_HEADS // 2
    i = pl.program_id(1)
    q0 = pl.multiple_of(i * bq, bq)
    lam = _lam_of(lam_ref)
    visible, before, dist_shift = _diag_tiles(bq, bq)
    col = lax.broadcasted_iota(jnp.int32, (1, bq), 1)
    tri = _strict_upper(bq)
    qa = [_split_halves(q_ref[0, :, h * LANES:(h + 1) * LANES]) for h in range(A_HEADS)]
    qb = [_split_halves(q_ref[0, :, A_WIDTH + p * LANES:A_WIDTH + (p + 1) * LANES]) for p in range(pairs)]

    def kv(rows, base, n):
        return kv_ref[0, rows, base + n * LANES:base + (n + 1) * LANES]

    rows = pl.ds(q0, bq)
    a_st = tuple(
        _a_update(_a_init(2 * bq),
                  jnp.where(visible, _dot_nt(qa[h], kv(rows, KV_KA, h)) + SLOPES_LOG2[h] * dist_shift, NEG_INF),
                  kv(rows, KV_VA, h))
        for h in range(A_HEADS))
    b_st = tuple(
        _b_block(_dot_nt(qb[p], kv(rows, KV_KB, p)), kv(rows, KV_VB, p), tri,
                 jnp.zeros((2 * bq, 1), F32), jnp.zeros((bq, LANES), F32), before, bq)
        for p in range(pairs))

    def body(jj, st):
        a_st, b_st = st
        k0 = pl.multiple_of((i - 1 - jj) * bq, bq)
        rows = pl.ds(k0, bq)
        rel = (col + (k0 - q0)).astype(F32)
        a_new = tuple(
            _a_update(a_st[h], _dot_nt(qa[h], kv(rows, KV_KA, h)) + SLOPES_LOG2[h] * rel, kv(rows, KV_VA, h))
            for h in range(A_HEADS))
        b_new = tuple(
            _b_block(_dot_nt(qb[p], kv(rows, KV_KB, p)), kv(rows, KV_VB, p), tri, b_st[p][0], b_st[p][1], None, bq)
            for p in range(pairs))
        return a_new, b_new

    a_st, b_st = lax.fori_loop(0, i, body, (a_st, b_st))
    for h in range(A_HEADS):
        o_ref[0, :, h * LANES:(h + 1) * LANES] = _a_finish(a_st[h], lam, ga_ref[...], bq).astype(o_ref.dtype)
    for p in range(pairs):
        o_ref[0, :, A_WIDTH + p * LANES:A_WIDTH + (p + 1) * LANES] = (
            _b_finish(b_st[p][1], gb_ref[...]).astype(o_ref.dtype))


def _attn_prompt(lam_p, ga, gb2, q, kv, bq):
    b, t, _ = q.shape
    small = lambda shape: pl.BlockSpec(shape, lambda bi, i: (0, 0))
    return pl.pallas_call(
        _attn_prompt_kernel,
        out_shape=jax.ShapeDtypeStruct((b, t, D_MODEL), BF16),
        grid=(b, t // bq),
        in_specs=[small((4, A_QK_DIM)), small((1, LANES)), small((1, LANES)),
                  pl.BlockSpec((1, bq, D_MODEL), lambda bi, i: (bi, i, 0)),
                  pl.BlockSpec((1, t, 2048), lambda bi, i: (bi, 0, 0))],
        out_specs=pl.BlockSpec((1, bq, D_MODEL), lambda bi, i: (bi, i, 0)),
        compiler_params=_params("arbitrary", "arbitrary"),
        name="attn_prompt",
    )(lam_p, ga, gb2, q, kv)


def _attn_sample_kernel(lam_ref, ga_ref, gb_ref, q_ref, kv_ref, cak_ref, cav_ref, cbk_ref, cbv_ref, o_ref):
    nq = q_ref.shape[1]
    past = cak_ref.shape[1] // A_HEADS
    pb = 256
    lam = _lam_of(lam_ref)
    visible, before, dist_shift = _diag_tiles(nq, nq)
    pcol = lax.broadcasted_iota(jnp.int32, (1, past), 1)
    tri_new = _strict_upper(nq)
    tri_past = _strict_upper(pb)

    for h in range(A_HEADS):
        cs = slice(h * LANES, (h + 1) * LANES)
        qs = _split_halves(q_ref[0, :, cs])
        slope = SLOPES_LOG2[h]
        head_rows = pl.ds(h, past, stride=A_HEADS)
        s_past = _dot_nt(qs, cak_ref[0, head_rows, :].astype(BF16)) + slope * (pcol - past).astype(F32)
        state = _a_update(_a_init(2 * nq), s_past, cav_ref[0, head_rows, :].astype(BF16))
        k16 = kv_ref[0, :, KV_KA + h * LANES:KV_KA + (h + 1) * LANES]
        v16 = kv_ref[0, :, KV_VA + h * LANES:KV_VA + (h + 1) * LANES]
        s_new = jnp.where(visible, _dot_nt(qs, k16) + slope * dist_shift, NEG_INF)
        state = _a_update(state, s_new, v16)
        o_ref[0, :, cs] = _a_finish(state, lam, ga_ref[...], nq).astype(o_ref.dtype)

    for p in range(B_HEADS // 2):
        cs = slice(p * LANES, (p + 1) * LANES)
        qs = _split_halves(q_ref[0, :, A_WIDTH + p * LANES:A_WIDTH + (p + 1) * LANES])
        k16 = kv_ref[0, :, KV_KB + p * LANES:KV_KB + (p + 1) * LANES]
        v16 = kv_ref[0, :, KV_VB + p * LANES:KV_VB + (p + 1) * LANES]
        carry, acc = _b_block(_dot_nt(qs, k16), v16, tri_new,
                              jnp.zeros((2 * nq, 1), F32), jnp.zeros((nq, LANES), F32), before, nq)
        for j in reversed(range(past // pb)):
            keys = slice(j * pb, (j + 1) * pb)
            k_t = cbk_ref[0, 2 * p:2 * p + 2, :, keys].reshape(LANES, pb).astype(BF16)
            v_t = cbv_ref[0, 2 * p:2 * p + 2, :, keys].reshape(LANES, pb).astype(BF16)
            carry, acc = _b_block(_dot(qs, k_t), v_t, tri_past, carry, acc, None, nq, v_transposed=True)
        o_ref[0, :, A_WIDTH + p * LANES:A_WIDTH + (p + 1) * LANES] = _b_finish(acc, gb_ref[...]).astype(o_ref.dtype)


def _attn_sample(lam_p, ga, gb2, q, kv, cak, cav, cbk, cbv):
    b, t, _ = q.shape
    past = cak.shape[1] // A_HEADS
    small = lambda shape: pl.BlockSpec(shape, lambda bi: (0, 0))
    per_b = lambda rows, width: pl.BlockSpec((1, rows, width), lambda bi: (bi, 0, 0))
    a_cache = per_b(past * A_HEADS, LANES)
    b_cache = pl.BlockSpec((1, B_HEADS, B_DIM, past), lambda bi: (bi, 0, 0, 0))
    return pl.pallas_call(
        _attn_sample_kernel,
        out_shape=jax.ShapeDtypeStruct((b, t, D_MODEL), BF16),
        grid=(b,),
        in_specs=[small((4, A_QK_DIM)), small((1, LANES)), small((1, LANES)),
                  per_b(t, D_MODEL), per_b(t, 2048), a_cache, a_cache, b_cache, b_cache],
        out_specs=per_b(t, D_MODEL),
        compiler_params=_params("arbitrary"),
        name="attn_sample",
    )(lam_p, ga, gb2, q, kv, cak, cav, cbk, cbv)


def _route(logits_t, bias_col):
    n = logits_t.shape[1]
    scores = jax.nn.sigmoid(logits_t)
    sel3 = (scores + bias_col).reshape(N_GROUPS, 8, n)
    eio = lax.broadcasted_iota(jnp.int32, (N_GROUPS, 8, n), 1)
    m1 = jnp.max(sel3, axis=1, keepdims=True)
    i1 = jnp.min(jnp.where(sel3 == m1, eio, 8), axis=1, keepdims=True)
    m2 = jnp.max(jnp.where(eio == i1, -jnp.inf, sel3), axis=1, keepdims=True)
    gscore = (m1 + m2).reshape(N_GROUPS, n)
    gio = lax.broadcasted_iota(jnp.int32, (N_GROUPS, n), 0)
    keep = jnp.zeros((N_GROUPS, n), F32)
    for _ in range(TOPK_GROUPS):
        g = jnp.max(gscore, axis=0, keepdims=True)
        gi = jnp.min(jnp.where(gscore == g, gio, N_GROUPS), axis=0, keepdims=True)
        hit = gio == gi
        keep = jnp.where(hit, 1.0, keep)
        gscore = jnp.where(hit, -jnp.inf, gscore)
    cur = jnp.where(keep.reshape(N_GROUPS, 1, n) > 0.0, sel3, NEG_INF).reshape(N_EXPERTS, n)
    xio = lax.broadcasted_iota(jnp.int32, (N_EXPERTS, n), 0)
    chosen = jnp.zeros((N_EXPERTS, n), F32)
    ids, raw = [], []
    for _ in range(TOP_K):
        mx = jnp.max(cur, axis=0, keepdims=True)
        ei = jnp.min(jnp.where(cur == mx, xio, N_EXPERTS), axis=0, keepdims=True)
        hit = xio == ei
        chosen = jnp.where(hit, 1.0, chosen)
        cur = jnp.where(hit, -jnp.inf, cur)
        ids.append(ei.astype(F32))
        raw.append(jnp.sum(jnp.where(hit, scores, 0.0), axis=0, keepdims=True))
    denom = raw[0]
    for r in raw[1:]:
        denom = denom + r
    wts = [r / denom * ROUTED_SCALE for r in raw]
    pad = jnp.zeros((8 - TOP_K, n), F32)
    return jnp.concatenate(ids + [pad] + wts + [pad], axis=0), chosen


def _outproj_kernel(o_ref, x_ref, g1_ref, sc_ref, sh_ref, wout_ref, wrt_ref, br_ref, x1_ref, h2_ref, rt_ref, cnt_ref):
    bb, tt, d = x_ref.shape
    n = bb * tt
    proj = _dot(o_ref[...].reshape(n, d), wout_ref[...]).reshape(bb, tt, d)
    x1 = x_ref[...] + g1_ref[...] * proj
    x1_ref[...] = x1
    h2 = (_rms(x1) * (1.0 + sc_ref[...]) + sh_ref[...]).reshape(n, d)
    h2hi = h2.astype(BF16)
    h2_ref[...] = h2hi.reshape(bb, tt, d)
    h2lo = (h2 - h2hi.astype(F32)).astype(BF16)
    w = wrt_ref[...]
    whi = w.astype(BF16)
    wlo = (w - whi.astype(F32)).astype(BF16)
    logits_t = _dot_nt(whi, h2hi) + _dot_nt(whi, h2lo) + _dot_nt(wlo, h2hi)
    rt, chosen = _route(logits_t, br_ref[...])
    rt_ref[...] = rt
    ones = jnp.ones((8, TM), BF16)
    for j in range(n // TM):
        cnt_ref[j] = _dot_nt(ones, chosen[:, j * TM:(j + 1) * TM].astype(BF16))


def _outproj(o, x, g1, sc, sh, wout_bf16, wr_t, br_col, bb, tt):
    b, t, d = x.shape
    n = bb * tt
    steps_t = t // tt
    tok = pl.BlockSpec((bb, tt, d), lambda i, j: (i, j, 0))
    mspec = pl.BlockSpec((bb, 1, d), lambda i, j: (i, 0, 0))
    full = lambda shape: pl.BlockSpec(shape, lambda i, j: (0, 0))
    return pl.pallas_call(
        _outproj_kernel,
        out_shape=(jax.ShapeDtypeStruct((b, t, d), F32), jax.ShapeDtypeStruct((b, t, d), BF16),
                   jax.ShapeDtypeStruct((16, b * t), F32),
                   jax.ShapeDtypeStruct((b * t // TM, 8, N_EXPERTS), F32)),
        grid=(b // bb, steps_t),
        in_specs=[tok, tok, mspec, mspec, mspec, full((d, d)), full((N_EXPERTS, d)), full((N_EXPERTS, 1))],
        out_specs=(tok, tok,
                   pl.BlockSpec((16, n), lambda i, j: (0, i * steps_t + j)),
                   pl.BlockSpec((n // TM, 8, N_EXPERTS), lambda i, j: (i * steps_t + j, 0, 0))),
        compiler_params=_params("arbitrary", "arbitrary"),
        name="outproj_router",
    )(o, x, g1, sc, sh, wout_bf16, wr_t, br_col)


COMMON_BITS = 2


def _seg_chunks(seg_ref, np_ref, base_ref, anchor, g, cap, visit):
    def body(e, c):
        idx = g * N_EXPERTS + e
        s0 = seg_ref[idx]
        n16 = np_ref[idx]
        b0 = e * cap + base_ref[idx]

        def chunk(b):
            off = (n16 & ((1 << b) - 1)) * SEG

            @pl.when(((n16 >> b) & 1) == 1)
            def _():
                visit(pl.multiple_of(s0 + off, SEG), pl.multiple_of(b0 + off, SEG), SEG << b)

        for b in range(COMMON_BITS):
            chunk(b)

        @pl.when(n16 >= (1 << COMMON_BITS))
        def _():
            anchor[...] = jnp.zeros_like(anchor)
            for b in range(COMMON_BITS, SEG_BITS):
                chunk(b)
        return c
    lax.fori_loop(0, N_EXPERTS, body, 0)


def _tile_rows(seg_ref, np_ref, g):
    last = g * N_EXPERTS + N_EXPERTS - 1
    return seg_ref[last] + np_ref[last] * SEG


def _wait_tile(seg_ref, np_ref, g, wait_rows):
    n16 = _tile_rows(seg_ref, np_ref, g) >> SEG_SHIFT
    for b in range(TILE_BITS):
        @pl.when(((n16 >> b) & 1) == 1)
        def _(b=b):
            wait_rows(SEG << b)


def _dispatch_kernel(seg_ref, np_ref, base_ref, fin_ref, rem_ref, rt_ref, h2_ref, *rest, t0, nsteps, cap, fill):
    xs_ref, pos_ref, buf, zbuf, anchor, sem, zsem = rest[-7:]
    s = pl.program_id(0)
    g = t0 + s
    slot = s % 2

    def copy(slot_, row, hrow, size):
        return pltpu.make_async_copy(buf.at[slot_, pl.ds(row, size)], xs_ref.at[pl.ds(hrow, size)], sem.at[slot_])

    def chunks(g_, fn):
        _seg_chunks(seg_ref, np_ref, base_ref, anchor, g_, cap, fn)

    def wait_tile(g_, slot_):
        _wait_tile(seg_ref, np_ref, g_, lambda n: copy(slot_, 0, 0, n).wait())

    @pl.when(s >= 2)
    def _():
        wait_tile(g - 2, slot)

    rt = rt_ref[...]
    eio = lax.broadcasted_iota(jnp.int32, (N_EXPERTS, TM), 0).astype(F32)
    chosen = jnp.zeros((N_EXPERTS, TM), F32)
    for k in range(TOP_K):
        chosen = jnp.where(eio == rt[k:k + 1], 1.0, chosen)
    t_row = lax.broadcasted_iota(jnp.int32, (TM, TM), 0)
    t_col = lax.broadcasted_iota(jnp.int32, (TM, TM), 1)
    rank = _dot(chosen.astype(BF16), jnp.where(t_row < t_col, 1.0, 0.0).astype(BF16))
    n_e = jnp.sum(chosen, axis=1, keepdims=True).astype(jnp.int32)
    np16 = ((n_e + (SEG - 1)) >> SEG_SHIFT).astype(F32)
    e_row = lax.broadcasted_iota(jnp.int32, (N_EXPERTS, N_EXPERTS), 0)
    e_col = lax.broadcasted_iota(jnp.int32, (N_EXPERTS, N_EXPERTS), 1)
    lower = jnp.where(e_col < e_row, 1.0, 0.0).astype(BF16)
    seg0 = SEG * _dot(lower, jnp.broadcast_to(np16, (N_EXPERTS, LANES)).astype(BF16))[:, 0:1]
    pos = seg0 + rank
    pos_rows = [jnp.sum(jnp.where(eio == rt[k:k + 1], pos, 0.0), axis=0, keepdims=True) for k in range(TOP_K)]
    pos_ref[...] = jnp.concatenate(pos_rows + [jnp.zeros((8 - TOP_K, TM), F32)], axis=0)

    h2 = h2_ref[...]
    npairs = (_tile_rows(seg_ref, np_ref, g) + (2 * CH - 1)) >> (CH_SHIFT + 1)

    def chunk_pair(c2, carry):
        for half in range(2):
            r0 = pl.multiple_of(c2 * (2 * CH) + half * CH, CH)
            rio = (lax.broadcasted_iota(jnp.int32, (CH, TM), 0) + r0).astype(F32)
            p = jnp.zeros((CH, TM), F32)
            for k in range(TOP_K):
                p = jnp.where(rio == pos_rows[k], 1.0, p)
            buf[slot, pl.ds(r0, CH), :] = _dot(p.astype(BF16), h2).astype(BF16)
        return carry
    lax.fori_loop(0, npairs, chunk_pair, 0)

    chunks(g, lambda r, hr, n: copy(slot, r, hr, n).start())

    @pl.when(s == nsteps - 1)
    def _():
        if nsteps >= 2:
            wait_tile(g - 1, 1 - slot)
        wait_tile(g, slot)
        if fill:
            zbuf[...] = jnp.zeros_like(zbuf)

            def zero_chunks(fn):
                def body(e, c):
                    r16 = rem_ref[e]
                    h0 = e * cap + fin_ref[e]
                    for b in range(FILL_BITS):
                        off = (r16 & ((1 << b) - 1)) * SEG

                        @pl.when(((r16 >> b) & 1) == 1)
                        def _(b=b, off=off):
                            size = SEG << b
                            fn(pltpu.make_async_copy(zbuf.at[pl.ds(0, size)],
                                                     xs_ref.at[pl.ds(pl.multiple_of(h0 + off, SEG), size)], zsem))
                    return c
                lax.fori_loop(0, N_EXPERTS, body, 0)
            zero_chunks(lambda cp: cp.start())
            zero_chunks(lambda cp: cp.wait())


def _dispatch(plan, rt, h2, xs, t0, cap, fill):
    n, d = h2.shape
    nsteps = n // TM
    kern = functools.partial(_dispatch_kernel, t0=t0, nsteps=nsteps, cap=cap, fill=fill)
    in_specs = [pl.BlockSpec((16, TM), lambda s, *_: (0, s)), pl.BlockSpec((TM, d), lambda s, *_: (s, 0))]
    args = [rt, h2]
    aliases = {}
    if xs is not None:
        in_specs.append(pl.BlockSpec(memory_space=pl.ANY))
        args.append(xs)
        aliases = {len(plan) + 2: 0}
    return pl.pallas_call(
        kern,
        out_shape=(jax.ShapeDtypeStruct((N_EXPERTS * cap, d), BF16), jax.ShapeDtypeStruct((8, n), F32)),
        grid_spec=pltpu.PrefetchScalarGridSpec(
            num_scalar_prefetch=len(plan), grid=(nsteps,), in_specs=in_specs,
            out_specs=(pl.BlockSpec(memory_space=pl.ANY), pl.BlockSpec((8, TM), lambda s, *_: (0, s))),
            scratch_shapes=[pltpu.VMEM((2, RMAX, d), BF16), pltpu.VMEM((TR // 2, d), BF16),
                            pltpu.VMEM((8, LANES), F32),
                            pltpu.SemaphoreType.DMA((2,)), pltpu.SemaphoreType.DMA(())]),
        input_output_aliases=aliases,
        compiler_params=_params("arbitrary"),
        name="dispatch",
    )(*plan, *args)


def _expert_kernel(te_ref, tb_ref, tv_ref, tf_ref, xs_ref, wg_ref, wu_ref, wd_ref, ys_ref, wg16, wu16, wd16):
    i = pl.program_id(0)

    @pl.when(tf_ref[i] == 1)
    def _():
        wg16[...] = wg_ref[0].astype(BF16)
        wu16[...] = wu_ref[0].astype(BF16)
        wd16[...] = wd_ref[0].astype(BF16)

    @pl.when(tv_ref[i] == 1)
    def _():
        x = xs_ref[...]
        act = _silu(_dot(x, wg16[...])) * _dot(x, wu16[...])
        ys_ref[...] = _dot(act.astype(BF16), wd16[...]).astype(BF16)


def _experts(sched, xs, wg, wu, wd, max_tiles):
    d = xs.shape[1]
    wspec = lambda shape: pl.BlockSpec((1,) + shape, lambda i, te, tb, tv, tf: (te[i], 0, 0))
    row_map = lambda i, te, tb, tv, tf: (tb[i], 0)
    return pl.pallas_call(
        _expert_kernel,
        out_shape=jax.ShapeDtypeStruct(xs.shape, BF16),
        grid_spec=pltpu.PrefetchScalarGridSpec(
            num_scalar_prefetch=4, grid=(max_tiles,),
            in_specs=[pl.BlockSpec((TR, d), row_map),
                      wspec((d, D_EXPERT)), wspec((d, D_EXPERT)), wspec((D_EXPERT, d))],
            out_specs=pl.BlockSpec((TR, d), row_map),
            scratch_shapes=[pltpu.VMEM((d, D_EXPERT), BF16), pltpu.VMEM((d, D_EXPERT), BF16),
                            pltpu.VMEM((D_EXPERT, d), BF16)]),
        compiler_params=_params("arbitrary"),
        name="experts",
    )(*sched, xs, wg, wu, wd)


def _combine_kernel(seg_ref, np_ref, base_ref, pos_ref, rt_ref, x1_ref, h2_ref, g2_ref, wsg_ref, wsu_ref, wsd_ref,
                    gf_ref, ys_ref, y_ref, ybuf, acc, anchor, sem, *, t0, nsteps, cap):
    bb, tt, d = x1_ref.shape
    s = pl.program_id(0)
    g = t0 + s
    slot = s % 2

    def copy(slot_, row, hrow, size):
        return pltpu.make_async_copy(ys_ref.at[pl.ds(hrow, size)], ybuf.at[slot_, pl.ds(row, size)], sem.at[slot_])

    def chunks(g_, fn):
        _seg_chunks(seg_ref, np_ref, base_ref, anchor, g_, cap, fn)

    @pl.when(s == 0)
    def _():
        ybuf[...] = jnp.zeros_like(ybuf)
        chunks(g, lambda r, hr, n: copy(0, r, hr, n).start())

    @pl.when(s + 1 < nsteps)
    def _():
        chunks(g + 1, lambda r, hr, n: copy(1 - slot, r, hr, n).start())

    _wait_tile(seg_ref, np_ref, g, lambda n: copy(slot, 0, 0, n).wait())

    stacked = jnp.concatenate([pos_ref[...], jnp.zeros((8, TM), F32), rt_ref[...],
                               jnp.zeros((LANES - 32, TM), F32)], axis=0)
    cols = stacked.T
    pos_b = [jnp.broadcast_to(cols[:, k:k + 1], (TM, LANES)) for k in range(TOP_K)]
    wt_b = [jnp.broadcast_to(cols[:, 24 + k:25 + k], (TM, LANES)) for k in range(TOP_K)]
    npairs = (_tile_rows(seg_ref, np_ref, g) + (2 * CH - 1)) >> (CH_SHIFT + 1)
    acc[...] = jnp.zeros_like(acc)

    def chunk_pair(c2, carry):
        part = None
        for half in range(2):
            r0 = pl.multiple_of(c2 * (2 * CH) + half * CH, CH)
            w_parts = []
            for lane0 in range(0, CH, LANES):
                rio = (lax.broadcasted_iota(jnp.int32, (TM, LANES), 1) + (r0 + lane0)).astype(F32)
                w = jnp.zeros((TM, LANES), F32)
                for k in range(TOP_K):
                    w = jnp.where(rio == pos_b[k], wt_b[k], w)
                w_parts.append(w.astype(BF16))
            prod = _dot(jnp.concatenate(w_parts, axis=1), ybuf[slot, pl.ds(r0, CH), :])
            part = prod if part is None else part + prod
        acc[...] += part
        return carry
    lax.fori_loop(0, npairs, chunk_pair, 0)

    h = h2_ref[...].reshape(bb * tt, d)
    act = _silu(_dot(h, wsg_ref[...])) * _dot(h, wsu_ref[...])
    moe = acc[...] + _dot(act.astype(BF16), wsd_ref[...])
    x2 = x1_ref[...] + g2_ref[...] * moe.reshape(bb, tt, d)
    y_ref[...] = _rms(x2) * gf_ref[...]


def _combine(plan3, pos, rt, x1, h2, g2, wsg, wsu, wsd, gf, ys, bb, tt, t0, cap):
    b, t, d = x1.shape
    steps_t = t // tt
    nsteps = (b // bb) * steps_t
    bidx = lambda s: (s // steps_t, s % steps_t, 0)
    tok = pl.BlockSpec((bb, tt, d), lambda s, *_: bidx(s))
    full = lambda shape: pl.BlockSpec(shape, lambda s, *_: (0,) * len(shape))
    kern = functools.partial(_combine_kernel, t0=t0, nsteps=nsteps, cap=cap)
    return pl.pallas_call(
        kern,
        out_shape=jax.ShapeDtypeStruct((b, t, d), F32),
        grid_spec=pltpu.PrefetchScalarGridSpec(
            num_scalar_prefetch=3, grid=(nsteps,),
            in_specs=[pl.BlockSpec((8, TM), lambda s, *_: (0, s)), pl.BlockSpec((16, TM), lambda s, *_: (0, s)),
                      tok, tok, pl.BlockSpec((bb, 1, d), lambda s, *_: (s // steps_t, 0, 0)),
                      full(wsg.shape), full(wsu.shape), full(wsd.shape), full((1, 1, d)),
                      pl.BlockSpec(memory_space=pl.ANY)],
            out_specs=tok,
            scratch_shapes=[pltpu.VMEM((2, RMAX, d), BF16), pltpu.VMEM((TM, d), F32),
                            pltpu.VMEM((8, LANES), F32), pltpu.SemaphoreType.DMA((2,))]),
        compiler_params=_params("arbitrary"),
        name="combine",
    )(*plan3, pos, rt, x1, h2, g2, wsg, wsu, wsd, gf, ys)


def _plan(cnt, cap, max_tiles):
    np16 = (cnt + (SEG - 1)) // SEG
    npad = np16 * SEG
    seg = jnp.cumsum(npad, axis=1) - npad
    base = jnp.cumsum(npad, axis=0) - npad
    fin = jnp.sum(npad, axis=0)
    rem16 = ((-fin) % TR) // SEG
    tiles_e = (fin + (TR - 1)) // TR
    ends = jnp.cumsum(tiles_e)
    i = jnp.arange(max_tiles, dtype=jnp.int32)
    valid = i < ends[-1]
    ii = jnp.where(valid, i, jnp.maximum(ends[-1] - 1, 0))
    te = jnp.sum((ends[None, :] <= ii[:, None]).astype(jnp.int32), axis=1)
    te = jnp.minimum(te, N_EXPERTS - 1)
    owner = jnp.arange(N_EXPERTS, dtype=jnp.int32)[None, :] == te[:, None]
    r = ii - jnp.sum(jnp.where(owner, (ends - tiles_e)[None, :], 0), axis=1)
    tb = te * (cap // TR) + r
    first = jnp.logical_and(valid, r == 0)
    i32 = lambda a: a.astype(jnp.int32).reshape(-1)
    return (i32(seg), i32(np16), i32(base), i32(fin), i32(rem16)), (te, i32(tb), i32(valid), i32(first))


def kernel(x_prompt, x_sample, cache_a_k, cache_a_v, cache_b_k, cache_b_v, c_prompt, c_sample, w_ada, b_ada, w_in, lam_params, g_sub_a, g_sub_b, w_out, w_router, b_router, w_gate, w_up, w_down, w_shared_gate, w_shared_up, w_shared_down, g_final):
    bp, tp, d = x_prompt.shape
    bs, ts, _ = x_sample.shape
    past = cache_a_k.shape[2]

    mod = _ada(jnp.concatenate([c_prompt, c_sample], axis=0), w_ada[0], b_ada[0])
    mod = mod.reshape(bp + bs, 6, 1, d)
    sh1, sc1, g1, sh2, sc2, g2 = (mod[:, k] for k in range(6))

    w_in16 = w_in[0].astype(BF16)
    w_out16 = w_out[0].astype(BF16)
    wr_t = w_router[0].T
    br_col = b_router[0].reshape(N_EXPERTS, 1)
    wsg16, wsu16, wsd16 = (w_shared_gate[0].astype(BF16), w_shared_up[0].astype(BF16),
                           w_shared_down[0].astype(BF16))
    ga = g_sub_a[0].reshape(1, LANES)
    gb2 = jnp.concatenate([g_sub_b[0], g_sub_b[0]]).reshape(1, LANES)
    gf = g_final.reshape(1, 1, d)
    lam_p = lam_params[0]

    def front(x, sl, bb, tt, attn):
        q, ka, va, kb, vb, kv = _inproj(x, sc1[sl], sh1[sl], w_in16, bb, tt)
        o = attn(q, kv)
        x1, h2, rt, cnt = _outproj(o, x, g1[sl], sc2[sl], sh2[sl], w_out16, wr_t, br_col, bb, tt)
        return (x1, h2, rt, cnt), (ka, va, kb, vb)

    mid_p, rows_p = front(x_prompt, slice(0, bp), 1, 512,
                          lambda q, kv: _attn_prompt(lam_p, ga, gb2, q, kv, 256))
    caches = (cache_a_k.reshape(bs, past * A_HEADS, LANES), cache_a_v.reshape(bs, past * A_HEADS, LANES),
              jnp.transpose(cache_b_k[0], (0, 2, 3, 1)), jnp.transpose(cache_b_v[0], (0, 2, 3, 1)))
    mid_s, rows_s = front(x_sample, slice(bp, bp + bs), 8, ts,
                          lambda q, kv: _attn_sample(lam_p, ga, gb2, q, kv, *caches))

    (x1_p, h2_p, rt_p, cnt_p), (x1_s, h2_s, rt_s, cnt_s) = mid_p, mid_s
    n_p, n_s = bp * tp, bs * ts
    tiles_p, tiles_s = n_p // TM, n_s // TM
    tiles = tiles_p + tiles_s
    cap = -(-(n_p + n_s + tiles * (SEG - 1)) // TR) * TR
    max_tiles = (TOP_K * (n_p + n_s) + tiles * N_EXPERTS * (SEG - 1)) // TR + N_EXPERTS
    cnt = jnp.concatenate([cnt_p[:, 0, :], cnt_s[:, 0, :]], axis=0).astype(jnp.int32)
    plan, sched = _plan(cnt, cap, max_tiles)
    xs, pos_p = _dispatch(plan, rt_p, h2_p.reshape(n_p, d), None, 0, cap, False)
    xs, pos_s = _dispatch(plan, rt_s, h2_s.reshape(n_s, d), xs, tiles_p, cap, True)
    ys = _experts(sched, xs, w_gate[0], w_up[0], w_down[0], max_tiles)
    y_p = _combine(plan[:3], pos_p, rt_p, x1_p, h2_p, g2[0:bp], wsg16, wsu16, wsd16, gf, ys,
                   1, TM, 0, cap)
    y_s = _combine(plan[:3], pos_s, rt_s, x1_s, h2_s, g2[bp:bp + bs], wsg16, wsu16, wsd16, gf, ys,
                   TM // ts, ts, tiles_p, cap)

    def shape_rows(rows, b, t):
        ka, va, kb, vb = rows
        return (ka.reshape(1, b, t, A_HEADS, 2 * A_QK_DIM), va.reshape(1, b, t, A_HEADS, 2 * A_QK_DIM),
                kb.reshape(1, b, t, B_HEADS, B_DIM), vb.reshape(1, b, t, B_HEADS, B_DIM))

    return (y_p, y_s) + shape_rows(rows_p, bp, tp) + shape_rows(rows_s, bs, ts)
```

```python
import functools
import math

import jax
import jax.numpy as jnp
from jax import lax
from jax.experimental import pallas as pl
from jax.experimental.pallas import tpu as pltpu

F32 = jnp.float32
BF16 = jnp.bfloat16

D_MODEL = 1024
CHUNK = 64
A_QK_DIM = 64
A_HEADS = 4
A_WIDTH = 512
B_DIM = 64
B_HEADS = 8
B_WIDTH = 512
IN_COLS = 3 * A_WIDTH + 3 * B_WIDTH
N_EXPERTS = 64
TOP_K = 6
N_GROUPS = 8
TOPK_GROUPS = 4
D_EXPERT = 256
ROUTED_SCALE = 2.5
EPS = 1e-6
NEG_INF = -1e30
LANES = 128
QK_SCALE = A_QK_DIM ** -0.5
LAM_INIT = 0.8 - 0.6 * math.exp(-0.3 * 0)
LOG2E = math.log2(math.e)
SLOPES_LOG2 = tuple(LOG2E * 2.0 ** (-8.0 * (h + 1.0) / A_HEADS) for h in range(A_HEADS))
KV_KA, KV_VA, KV_KB, KV_VB = 0, 512, 1024, 1536
VMEM_LIMIT = 56 * 1024 * 1024
TM = 256
SEG_SHIFT = 4
SEG = 1 << SEG_SHIFT
SEG_BITS = 5
CH_SHIFT = 8
CH = 1 << CH_SHIFT
RMAX = 2560
TILE_BITS = 8
TR = 1024
FILL_BITS = 6

_NT = (((1,), (1,)), ((), ()))


def _dot(a, b):
    return jnp.dot(a, b, preferred_element_type=F32)


def _dot_nt(a, b):
    return lax.dot_general(a, b, _NT, preferred_element_type=F32)


def _rms(x):
    return x * lax.rsqrt(jnp.mean(x * x, axis=-1, keepdims=True) + EPS)


def _silu(x):
    return x * jax.nn.sigmoid(x)


def _params(*sem):
    return pltpu.CompilerParams(dimension_semantics=sem, vmem_limit_bytes=VMEM_LIMIT)


def _ada_kernel(c_ref, w_ref, b_ref, o_ref):
    s = _silu(c_ref[...]).astype(BF16)
    o_ref[...] = _dot(s, w_ref[...].astype(BF16)) + b_ref[...]


def _ada(c, w, b):
    n, d = c.shape
    cols = w.shape[1]
    tn = 1536
    return pl.pallas_call(
        _ada_kernel,
        out_shape=jax.ShapeDtypeStruct((n, cols), F32),
        grid=(cols // tn,),
        in_specs=[pl.BlockSpec((n, d), lambda j: (0, 0)),
                  pl.BlockSpec((d, tn), lambda j: (0, j)),
                  pl.BlockSpec((1, tn), lambda j: (0, j))],
        out_specs=pl.BlockSpec((n, tn), lambda j: (0, j)),
        compiler_params=_params("arbitrary"),
        name="ada",
    )(c, w, b.reshape(1, cols))


def _inproj_kernel(x_ref, sc_ref, sh_ref, w_ref, q_ref, ka_ref, va_ref, kb_ref, vb_ref, kv_ref):
    bb, tt, d = x_ref.shape
    h = _rms(x_ref[...]) * (1.0 + sc_ref[...]) + sh_ref[...]
    proj = _dot(h.reshape(bb * tt, d).astype(BF16), w_ref[...])

    def part(lo, width):
        return proj[:, lo:lo + width].reshape(bb, tt, width)

    qa, ka, va = part(0, 512), part(512, 512), part(1024, 512)
    qb, kb, vb = part(1536, 512), part(2048, 512), part(2560, 512)
    q_ref[:, :, 0:512] = (qa * (QK_SCALE * LOG2E)).astype(BF16)
    q_ref[:, :, 512:1024] = (qb * (QK_SCALE * LOG2E)).astype(BF16)
    for hd in range(A_HEADS):
        ka_ref[:, pl.ds(hd, tt, stride=A_HEADS), :] = ka[:, :, hd * LANES:(hd + 1) * LANES]
        va_ref[:, pl.ds(hd, tt, stride=A_HEADS), :] = va[:, :, hd * LANES:(hd + 1) * LANES]
    kb_ref[...] = kb
    vb_ref[...] = vb
    kv_ref[:, :, KV_KA:KV_KA + 512] = ka.astype(BF16)
    kv_ref[:, :, KV_VA:KV_VA + 512] = va.astype(BF16)
    kv_ref[:, :, KV_KB:KV_KB + 512] = kb.astype(BF16)
    kv_ref[:, :, KV_VB:KV_VB + 512] = vb.astype(BF16)


def _inproj(x, sc, sh, w_bf16, bb, tt):
    b, t, d = x.shape
    xspec = pl.BlockSpec((bb, tt, d), lambda i, j: (i, j, 0))
    mspec = pl.BlockSpec((bb, 1, d), lambda i, j: (i, 0, 0))

    def ospec(width):
        return pl.BlockSpec((bb, tt, width), lambda i, j: (i, j, 0))

    f32rows = jax.ShapeDtypeStruct((b, t, 512), F32)
    a_rows = jax.ShapeDtypeStruct((b, t * A_HEADS, LANES), F32)
    a_spec = pl.BlockSpec((bb, tt * A_HEADS, LANES), lambda i, j: (i, j, 0))
    return pl.pallas_call(
        _inproj_kernel,
        out_shape=(jax.ShapeDtypeStruct((b, t, 1024), BF16), a_rows, a_rows, f32rows, f32rows,
                   jax.ShapeDtypeStruct((b, t, 2048), BF16)),
        grid=(b // bb, t // tt),
        in_specs=[xspec, mspec, mspec, pl.BlockSpec((d, IN_COLS), lambda i, j: (0, 0))],
        out_specs=(ospec(1024), a_spec, a_spec, ospec(512), ospec(512), ospec(2048)),
        compiler_params=_params("arbitrary", "arbitrary"),
        name="inproj",
    )(x, sc, sh, w_bf16)


def _lane_lo(shape):
    return lax.broadcasted_iota(jnp.int32, shape, len(shape) - 1) < 64


def _split_halves(x):
    lo = _lane_lo(x.shape)
    zero = jnp.zeros_like(x)
    return jnp.concatenate([jnp.where(lo, x, zero), jnp.where(lo, zero, x)], axis=0)


def _lam_of(lam_ref):
    lp = lam_ref[...]
    s1 = jnp.sum(lp[0:1] * lp[1:2], axis=1, keepdims=True)
    s2 = jnp.sum(lp[2:3] * lp[3:4], axis=1, keepdims=True)
    return jnp.exp(s1) - jnp.exp(s2) + LAM_INIT


def _a_update(state, s, v16):
    m, l, acc = state
    m_new = jnp.maximum(m, jnp.max(s, axis=1, keepdims=True))
    alpha = jnp.exp2(m - m_new)
    p = jnp.exp2(s - m_new)
    l = alpha * l + jnp.sum(p, axis=1, keepdims=True)
    acc = alpha * acc + _dot(p.astype(BF16), v16)
    return m_new, l, acc


def _a_init(rows):
    return (jnp.full((rows, 1), NEG_INF, F32), jnp.zeros((rows, 1), F32), jnp.zeros((rows, LANES), F32))


def _a_finish(state, lam, gain, nq):
    _, l, acc = state
    o = acc[:nq] / l[:nq] - lam * (acc[nq:] / l[nq:])
    return _rms(o) * gain * (1.0 - LAM_INIT)


def _diag_tiles(nq, nk):
    r = lax.broadcasted_iota(jnp.int32, (2 * nq, nk), 0)
    r = jnp.where(r >= nq, r - nq, r)
    c = lax.broadcasted_iota(jnp.int32, (2 * nq, nk), 1)
    visible = (c // CHUNK) <= (r // CHUNK)
    before = c < r
    dist_shift = (r - jnp.abs(r - c)).astype(F32)
    return visible, before, dist_shift


def _strict_upper(n):
    j = lax.broadcasted_iota(jnp.int32, (n, n), 0)
    s = lax.broadcasted_iota(jnp.int32, (n, n), 1)
    return jnp.where(j > s, 1.0, 0.0).astype(BF16)


def _b_block(z, v16, tri, carry, acc, before, nq, v_transposed=False):
    sp = jnp.where(z > 60.0, z, jnp.log2(1.0 + jnp.exp2(z)))
    if before is not None:
        sp = jnp.where(before, sp, 0.0)
    tk = tri.shape[0]
    sp16 = sp.astype(BF16)
    pieces, right = [], None
    for c in reversed(range(z.shape[1] // tk)):
        piece = _dot(sp16[:, c * tk:(c + 1) * tk], tri)
        pieces.append(piece if right is None else piece + right)
        chunk_sum = jnp.sum(sp[:, c * tk:(c + 1) * tk], axis=1, keepdims=True)
        right = chunk_sum if right is None else right + chunk_sum
    later = pieces[0] if len(pieces) == 1 else jnp.concatenate(pieces[::-1], axis=1)
    a = jnp.exp2((z - sp) - (carry + later))
    if before is not None:
        a = jnp.where(before, a, 0.0)
    a16 = a.astype(BF16)
    vzero = jnp.zeros_like(v16)
    if v_transposed:
        first = lax.broadcasted_iota(jnp.int32, v16.shape, 0) < B_DIM
        acc = acc + _dot_nt(a16[:nq], jnp.where(first, v16, vzero)) + _dot_nt(a16[nq:], jnp.where(first, vzero, v16))
    else:
        vlo = _lane_lo(v16.shape)
        acc = acc + _dot(a16[:nq], jnp.where(vlo, v16, vzero)) + _dot(a16[nq:], jnp.where(vlo, vzero, v16))
    return carry + right, acc


def _b_finish(acc, gain2):
    lo = _lane_lo(acc.shape)
    ss = acc * acc
    s_lo = jnp.sum(jnp.where(lo, ss, 0.0), axis=1, keepdims=True)
    s_hi = jnp.sum(jnp.where(lo, 0.0, ss), axis=1, keepdims=True)
    ms = jnp.where(lo, s_lo, s_hi) * (1.0 / B_DIM)
    return acc * lax.rsqrt(ms + EPS) * gain2


def _attn_prompt_kernel(lam_ref, ga_ref, gb_ref, q_ref, kv_ref, o_ref):
    bq = q_ref.shape[1]
    pairs = B_HEADS // 2
    i = pl.program_id(1)
    q0 = pl.multiple_of(i * bq, bq)
    lam = _lam_of(lam_ref)
    visible, before, dist_shift = _diag_tiles(bq, bq)
    col = lax.broadcasted_iota(jnp.int32, (1, bq), 1)
    tri = _strict_upper(min(bq, 256))
    qa = [_split_halves(q_ref[0, :, h * LANES:(h + 1) * LANES]) for h in range(A_HEADS)]
    qb = [_split_halves(q_ref[0, :, A_WIDTH + p * LANES:A_WIDTH + (p + 1) * LANES]) for p in range(pairs)]

    def kv(rows, base, n):
        return kv_ref[0, rows, base + n * LANES:base + (n + 1) * LANES]

    rows = pl.ds(q0, bq)
    a_st = tuple(
        _a_update(_a_init(2 * bq),
                  jnp.where(visible, _dot_nt(qa[h], kv(rows, KV_KA, h)) + SLOPES_LOG2[h] * dist_shift, NEG_INF),
                  kv(rows, KV_VA, h))
        for h in range(A_HEADS))
    b_st = tuple(
        _b_block(_dot_nt(qb[p], kv(rows, KV_KB, p)), kv(rows, KV_VB, p), tri,
                 jnp.zeros((2 * bq, 1), F32), jnp.zeros((bq, LANES), F32), before, bq)
        for p in range(pairs))

    def body(jj, st):
        a_st, b_st = st
        k0 = pl.multiple_of((i - 1 - jj) * bq, bq)
        rows = pl.ds(k0, bq)
        rel = (col + (k0 - q0)).astype(F32)
        a_new = tuple(
            _a_update(a_st[h], _dot_nt(qa[h], kv(rows, KV_KA, h)) + SLOPES_LOG2[h] * rel, kv(rows, KV_VA, h))
            for h in range(A_HEADS))
        b_new = tuple(
            _b_block(_dot_nt(qb[p], kv(rows, KV_KB, p)), kv(rows, KV_VB, p), tri, b_st[p][0], b_st[p][1], None, bq)
            for p in range(pairs))
        return a_new, b_new

    a_st, b_st = lax.fori_loop(0, i, body, (a_st, b_st))
    for h in range(A_HEADS):
        o_ref[0, :, h * LANES:(h + 1) * LANES] = _a_finish(a_st[h], lam, ga_ref[...], bq).astype(o_ref.dtype)
    for p in range(pairs):
        o_ref[0, :, A_WIDTH + p * LANES:A_WIDTH + (p + 1) * LANES] = (
            _b_finish(b_st[p][1], gb_ref[...]).astype(o_ref.dtype))


def _attn_prompt(lam_p, ga, gb2, q, kv, bq):
    b, t, _ = q.shape
    small = lambda shape: pl.BlockSpec(shape, lambda bi, i: (0, 0))
    return pl.pallas_call(
        _attn_prompt_kernel,
        out_shape=jax.ShapeDtypeStruct((b, t, D_MODEL), BF16),
        grid=(b, t // bq),
        in_specs=[small((4, A_QK_DIM)), small((1, LANES)), small((1, LANES)),
                  pl.BlockSpec((1, bq, D_MODEL), lambda bi, i: (bi, i, 0)),
                  pl.BlockSpec((1, t, 2048), lambda bi, i: (bi, 0, 0), pipeline_mode=pl.Buffered(1))],
        out_specs=pl.BlockSpec((1, bq, D_MODEL), lambda bi, i: (bi, i, 0)),
        compiler_params=_params("arbitrary", "arbitrary"),
        name="attn_prompt",
    )(lam_p, ga, gb2, q, kv)


def _attn_sample_kernel(lam_ref, ga_ref, gb_ref, q_ref, kv_ref, cak_ref, cav_ref, cbk_ref, cbv_ref, o_ref):
    nq = q_ref.shape[1]
    past = cak_ref.shape[1] // A_HEADS
    pb = 256
    lam = _lam_of(lam_ref)
    visible, before, dist_shift = _diag_tiles(nq, nq)
    pcol = lax.broadcasted_iota(jnp.int32, (1, past), 1)
    tri_new = _strict_upper(nq)
    tri_past = _strict_upper(pb)

    for h in range(A_HEADS):
        cs = slice(h * LANES, (h + 1) * LANES)
        qs = _split_halves(q_ref[0, :, cs])
        slope = SLOPES_LOG2[h]
        head_rows = pl.ds(h, past, stride=A_HEADS)
        s_past = _dot_nt(qs, cak_ref[0, head_rows, :].astype(BF16)) + slope * (pcol - past).astype(F32)
        state = _a_update(_a_init(2 * nq), s_past, cav_ref[0, head_rows, :].astype(BF16))
        k16 = kv_ref[0, :, KV_KA + h * LANES:KV_KA + (h + 1) * LANES]
        v16 = kv_ref[0, :, KV_VA + h * LANES:KV_VA + (h + 1) * LANES]
        s_new = jnp.where(visible, _dot_nt(qs, k16) + slope * dist_shift, NEG_INF)
        state = _a_update(state, s_new, v16)
        o_ref[0, :, cs] = _a_finish(state, lam, ga_ref[...], nq).astype(o_ref.dtype)

    for p in range(B_HEADS // 2):
        cs = slice(p * LANES, (p + 1) * LANES)
        qs = _split_halves(q_ref[0, :, A_WIDTH + p * LANES:A_WIDTH + (p + 1) * LANES])
        k16 = kv_ref[0, :, KV_KB + p * LANES:KV_KB + (p + 1) * LANES]
        v16 = kv_ref[0, :, KV_VB + p * LANES:KV_VB + (p + 1) * LANES]
        carry, acc = _b_block(_dot_nt(qs, k16), v16, tri_new,
                              jnp.zeros((2 * nq, 1), F32), jnp.zeros((nq, LANES), F32), before, nq)
        for j in reversed(range(past // pb)):
            keys = slice(j * pb, (j + 1) * pb)
            k_t = cbk_ref[0, 2 * p:2 * p + 2, :, keys].reshape(LANES, pb).astype(BF16)
            v_t = cbv_ref[0, 2 * p:2 * p + 2, :, keys].reshape(LANES, pb).astype(BF16)
            carry, acc = _b_block(_dot(qs, k_t), v_t, tri_past, carry, acc, None, nq, v_transposed=True)
        o_ref[0, :, A_WIDTH + p * LANES:A_WIDTH + (p + 1) * LANES] = _b_finish(acc, gb_ref[...]).astype(o_ref.dtype)


def _attn_sample(lam_p, ga, gb2, q, kv, cak, cav, cbk, cbv):
    b, t, _ = q.shape
    past = cak.shape[1] // A_HEADS
    small = lambda shape: pl.BlockSpec(shape, lambda bi: (0, 0))
    per_b = lambda rows, width: pl.BlockSpec((1, rows, width), lambda bi: (bi, 0, 0))
    a_cache = per_b(past * A_HEADS, LANES)
    b_cache = pl.BlockSpec((1, B_HEADS, B_DIM, past), lambda bi: (bi, 0, 0, 0))
    return pl.pallas_call(
        _attn_sample_kernel,
        out_shape=jax.ShapeDtypeStruct((b, t, D_MODEL), BF16),
        grid=(b,),
        in_specs=[small((4, A_QK_DIM)), small((1, LANES)), small((1, LANES)),
                  per_b(t, D_MODEL), per_b(t, 2048), a_cache, a_cache, b_cache, b_cache],
        out_specs=per_b(t, D_MODEL),
        compiler_params=_params("arbitrary"),
        name="attn_sample",
    )(lam_p, ga, gb2, q, kv, cak, cav, cbk, cbv)


def _route(logits_t, bias_col):
    n = logits_t.shape[1]
    scores = jax.nn.sigmoid(logits_t)
    sel3 = (scores + bias_col).reshape(N_GROUPS, 8, n)
    eio = lax.broadcasted_iota(jnp.int32, (N_GROUPS, 8, n), 1)
    m1 = jnp.max(sel3, axis=1, keepdims=True)
    i1 = jnp.min(jnp.where(sel3 == m1, eio, 8), axis=1, keepdims=True)
    m2 = jnp.max(jnp.where(eio == i1, -jnp.inf, sel3), axis=1, keepdims=True)
    gscore = (m1 + m2).reshape(N_GROUPS, n)
    gio = lax.broadcasted_iota(jnp.int32, (N_GROUPS, n), 0)
    keep = jnp.zeros((N_GROUPS, n), F32)
    for _ in range(TOPK_GROUPS):
        g = jnp.max(gscore, axis=0, keepdims=True)
        gi = jnp.min(jnp.where(gscore == g, gio, N_GROUPS), axis=0, keepdims=True)
        hit = gio == gi
        keep = jnp.where(hit, 1.0, keep)
        gscore = jnp.where(hit, -jnp.inf, gscore)
    cur = jnp.where(keep.reshape(N_GROUPS, 1, n) > 0.0, sel3, NEG_INF).reshape(N_EXPERTS, n)
    xio = lax.broadcasted_iota(jnp.int32, (N_EXPERTS, n), 0)
    chosen = jnp.zeros((N_EXPERTS, n), F32)
    ids, raw = [], []
    for _ in range(TOP_K):
        mx = jnp.max(cur, axis=0, keepdims=True)
        ei = jnp.min(jnp.where(cur == mx, xio, N_EXPERTS), axis=0, keepdims=True)
        hit = xio == ei
        chosen = jnp.where(hit, 1.0, chosen)
        cur = jnp.where(hit, -jnp.inf, cur)
        ids.append(ei.astype(F32))
        raw.append(jnp.sum(jnp.where(hit, scores, 0.0), axis=0, keepdims=True))
    denom = raw[0]
    for r in raw[1:]:
        denom = denom + r
    wts = [r / denom * ROUTED_SCALE for r in raw]
    pad = jnp.zeros((8 - TOP_K, n), F32)
    return jnp.concatenate(ids + [pad] + wts + [pad], axis=0), chosen


def _outproj_kernel(o_ref, x_ref, g1_ref, sc_ref, sh_ref, wout_ref, wrt_ref, br_ref, x1_ref, h2_ref, rt_ref, cnt_ref):
    bb, tt, d = x_ref.shape
    n = bb * tt
    proj = _dot(o_ref[...].reshape(n, d), wout_ref[...]).reshape(bb, tt, d)
    x1 = x_ref[...] + g1_ref[...] * proj
    x1_ref[...] = x1
    h2 = (_rms(x1) * (1.0 + sc_ref[...]) + sh_ref[...]).reshape(n, d)
    h2hi = h2.astype(BF16)
    h2_ref[...] = h2hi.reshape(bb, tt, d)
    h2lo = (h2 - h2hi.astype(F32)).astype(BF16)
    w = wrt_ref[...]
    whi = w.astype(BF16)
    wlo = (w - whi.astype(F32)).astype(BF16)
    logits_t = _dot_nt(whi, h2hi) + _dot_nt(whi, h2lo) + _dot_nt(wlo, h2hi)
    rt, chosen = _route(logits_t, br_ref[...])
    rt_ref[...] = rt
    ones = jnp.ones((8, TM), BF16)
    for j in range(n // TM):
        cnt_ref[j] = _dot_nt(ones, chosen[:, j * TM:(j + 1) * TM].astype(BF16))


def _outproj(o, x, g1, sc, sh, wout_bf16, wr_t, br_col, bb, tt):
    b, t, d = x.shape
    n = bb * tt
    steps_t = t // tt
    tok = pl.BlockSpec((bb, tt, d), lambda i, j: (i, j, 0))
    mspec = pl.BlockSpec((bb, 1, d), lambda i, j: (i, 0, 0))
    full = lambda shape: pl.BlockSpec(shape, lambda i, j: (0, 0))
    return pl.pallas_call(
        _outproj_kernel,
        out_shape=(jax.ShapeDtypeStruct((b, t, d), F32), jax.ShapeDtypeStruct((b, t, d), BF16),
                   jax.ShapeDtypeStruct((16, b * t), F32),
                   jax.ShapeDtypeStruct((b * t // TM, 8, N_EXPERTS), F32)),
        grid=(b // bb, steps_t),
        in_specs=[tok, tok, mspec, mspec, mspec, full((d, d)), full((N_EXPERTS, d)), full((N_EXPERTS, 1))],
        out_specs=(tok, tok,
                   pl.BlockSpec((16, n), lambda i, j: (0, i * steps_t + j)),
                   pl.BlockSpec((n // TM, 8, N_EXPERTS), lambda i, j: (i * steps_t + j, 0, 0))),
        compiler_params=_params("arbitrary", "arbitrary"),
        name="outproj_router",
    )(o, x, g1, sc, sh, wout_bf16, wr_t, br_col)


COMMON_BITS = 2


def _seg_chunks(seg_ref, np_ref, base_ref, anchor, g, cap, visit):
    def body(e, c):
        idx = g * N_EXPERTS + e
        s0 = seg_ref[idx]
        n16 = np_ref[idx]
        b0 = e * cap + base_ref[idx]

        def chunk(b):
            off = (n16 & ((1 << b) - 1)) * SEG

            @pl.when(((n16 >> b) & 1) == 1)
            def _():
                visit(pl.multiple_of(s0 + off, SEG), pl.multiple_of(b0 + off, SEG), SEG << b)

        for b in range(COMMON_BITS):
            chunk(b)

        @pl.when(n16 >= (1 << COMMON_BITS))
        def _():
            anchor[...] = jnp.zeros_like(anchor)
            for b in range(COMMON_BITS, SEG_BITS):
                chunk(b)
        return c
    lax.fori_loop(0, N_EXPERTS, body, 0)


def _tile_rows(seg_ref, np_ref, g):
    last = g * N_EXPERTS + N_EXPERTS - 1
    return seg_ref[last] + np_ref[last] * SEG


def _wait_tile(seg_ref, np_ref, g, wait_rows):
    n16 = _tile_rows(seg_ref, np_ref, g) >> SEG_SHIFT
    for b in range(TILE_BITS):
        @pl.when(((n16 >> b) & 1) == 1)
        def _(b=b):
            wait_rows(SEG << b)


def _dispatch_kernel(seg_ref, np_ref, base_ref, fin_ref, rem_ref, rt_ref, h2_ref, *rest, t0, nsteps, cap, fill):
    xs_ref, pos_ref, buf, zbuf, anchor, sem, zsem = rest[-7:]
    s = pl.program_id(0)
    g = t0 + s
    slot = s % 2

    def copy(slot_, row, hrow, size):
        return pltpu.make_async_copy(buf.at[slot_, pl.ds(row, size)], xs_ref.at[pl.ds(hrow, size)], sem.at[slot_])

    def chunks(g_, fn):
        _seg_chunks(seg_ref, np_ref, base_ref, anchor, g_, cap, fn)

    def wait_tile(g_, slot_):
        _wait_tile(seg_ref, np_ref, g_, lambda n: copy(slot_, 0, 0, n).wait())

    @pl.when(s >= 2)
    def _():
        wait_tile(g - 2, slot)

    rt = rt_ref[...]
    eio = lax.broadcasted_iota(jnp.int32, (N_EXPERTS, TM), 0).astype(F32)
    chosen = jnp.zeros((N_EXPERTS, TM), F32)
    for k in range(TOP_K):
        chosen = jnp.where(eio == rt[k:k + 1], 1.0, chosen)
    t_row = lax.broadcasted_iota(jnp.int32, (TM, TM), 0)
    t_col = lax.broadcasted_iota(jnp.int32, (TM, TM), 1)
    rank = _dot(chosen.astype(BF16), jnp.where(t_row < t_col, 1.0, 0.0).astype(BF16))
    n_e = jnp.sum(chosen, axis=1, keepdims=True).astype(jnp.int32)
    np16 = ((n_e + (SEG - 1)) >> SEG_SHIFT).astype(F32)
    e_row = lax.broadcasted_iota(jnp.int32, (N_EXPERTS, N_EXPERTS), 0)
    e_col = lax.broadcasted_iota(jnp.int32, (N_EXPERTS, N_EXPERTS), 1)
    lower = jnp.where(e_col < e_row, 1.0, 0.0).astype(BF16)
    seg0 = SEG * _dot(lower, jnp.broadcast_to(np16, (N_EXPERTS, LANES)).astype(BF16))[:, 0:1]
    pos = seg0 + rank
    pos_rows = [jnp.sum(jnp.where(eio == rt[k:k + 1], pos, 0.0), axis=0, keepdims=True) for k in range(TOP_K)]
    pos_ref[...] = jnp.concatenate(pos_rows + [jnp.zeros((8 - TOP_K, TM), F32)], axis=0)

    h2 = h2_ref[...]
    npairs = (_tile_rows(seg_ref, np_ref, g) + (2 * CH - 1)) >> (CH_SHIFT + 1)

    def chunk_pair(c2, carry):
        for half in range(2):
            r0 = pl.multiple_of(c2 * (2 * CH) + half * CH, CH)
            rio = (lax.broadcasted_iota(jnp.int32, (CH, TM), 0) + r0).astype(F32)
            p = jnp.zeros((CH, TM), F32)
            for k in range(TOP_K):
                p = jnp.where(rio == pos_rows[k], 1.0, p)
            buf[slot, pl.ds(r0, CH), :] = _dot(p.astype(BF16), h2).astype(BF16)
        return carry
    lax.fori_loop(0, npairs, chunk_pair, 0)

    chunks(g, lambda r, hr, n: copy(slot, r, hr, n).start())

    @pl.when(s == nsteps - 1)
    def _():
        if nsteps >= 2:
            wait_tile(g - 1, 1 - slot)
        wait_tile(g, slot)
        if fill:
            zbuf[...] = jnp.zeros_like(zbuf)

            def zero_chunks(fn):
                def body(e, c):
                    r16 = rem_ref[e]
                    h0 = e * cap + fin_ref[e]
                    for b in range(FILL_BITS):
                        off = (r16 & ((1 << b) - 1)) * SEG

                        @pl.when(((r16 >> b) & 1) == 1)
                        def _(b=b, off=off):
                            size = SEG << b
                            fn(pltpu.make_async_copy(zbuf.at[pl.ds(0, size)],
                                                     xs_ref.at[pl.ds(pl.multiple_of(h0 + off, SEG), size)], zsem))
                    return c
                lax.fori_loop(0, N_EXPERTS, body, 0)
            zero_chunks(lambda cp: cp.start())
            zero_chunks(lambda cp: cp.wait())


def _dispatch(plan, rt, h2, xs, t0, cap, fill):
    n, d = h2.shape
    nsteps = n // TM
    kern = functools.partial(_dispatch_kernel, t0=t0, nsteps=nsteps, cap=cap, fill=fill)
    in_specs = [pl.BlockSpec((16, TM), lambda s, *_: (0, s)), pl.BlockSpec((TM, d), lambda s, *_: (s, 0))]
    args = [rt, h2]
    aliases = {}
    if xs is not None:
        in_specs.append(pl.BlockSpec(memory_space=pl.ANY))
        args.append(xs)
        aliases = {len(plan) + 2: 0}
    return pl.pallas_call(
        kern,
        out_shape=(jax.ShapeDtypeStruct((N_EXPERTS * cap, d), BF16), jax.ShapeDtypeStruct((8, n), F32)),
        grid_spec=pltpu.PrefetchScalarGridSpec(
            num_scalar_prefetch=len(plan), grid=(nsteps,), in_specs=in_specs,
            out_specs=(pl.BlockSpec(memory_space=pl.ANY), pl.BlockSpec((8, TM), lambda s, *_: (0, s))),
            scratch_shapes=[pltpu.VMEM((2, RMAX, d), BF16), pltpu.VMEM((TR // 2, d), BF16),
                            pltpu.VMEM((8, LANES), F32),
                            pltpu.SemaphoreType.DMA((2,)), pltpu.SemaphoreType.DMA(())]),
        input_output_aliases=aliases,
        compiler_params=_params("arbitrary"),
        name="dispatch",
    )(*plan, *args)


def _expert_kernel(te_ref, tb_ref, tv_ref, tf_ref, xs_ref, wg_ref, wu_ref, wd_ref, ys_ref, wg16, wu16, wd16):
    i = pl.program_id(0)

    @pl.when(tf_ref[i] == 1)
    def _():
        wg16[...] = wg_ref[0].astype(BF16)
        wu16[...] = wu_ref[0].astype(BF16)
        wd16[...] = wd_ref[0].astype(BF16)

    @pl.when(tv_ref[i] == 1)
    def _():
        x = xs_ref[...]
        act = _silu(_dot(x, wg16[...])) * _dot(x, wu16[...])
        ys_ref[...] = _dot(act.astype(BF16), wd16[...]).astype(BF16)


def _experts(sched, xs, wg, wu, wd, max_tiles):
    d = xs.shape[1]
    wspec = lambda shape: pl.BlockSpec((1,) + shape, lambda i, te, tb, tv, tf: (te[i], 0, 0))
    row_map = lambda i, te, tb, tv, tf: (tb[i], 0)
    return pl.pallas_call(
        _expert_kernel,
        out_shape=jax.ShapeDtypeStruct(xs.shape, BF16),
        grid_spec=pltpu.PrefetchScalarGridSpec(
            num_scalar_prefetch=4, grid=(max_tiles,),
            in_specs=[pl.BlockSpec((TR, d), row_map),
                      wspec((d, D_EXPERT)), wspec((d, D_EXPERT)), wspec((D_EXPERT, d))],
            out_specs=pl.BlockSpec((TR, d), row_map),
            scratch_shapes=[pltpu.VMEM((d, D_EXPERT), BF16), pltpu.VMEM((d, D_EXPERT), BF16),
                            pltpu.VMEM((D_EXPERT, d), BF16)]),
        compiler_params=_params("arbitrary"),
        name="experts",
    )(*sched, xs, wg, wu, wd)


def _combine_kernel(seg_ref, np_ref, base_ref, pos_ref, rt_ref, x1_ref, h2_ref, g2_ref, wsg_ref, wsu_ref, wsd_ref,
                    gf_ref, ys_ref, y_ref, ybuf, acc, anchor, sem, *, t0, nsteps, cap):
    bb, tt, d = x1_ref.shape
    s = pl.program_id(0)
    g = t0 + s
    slot = s % 2

    def copy(slot_, row, hrow, size):
        return pltpu.make_async_copy(ys_ref.at[pl.ds(hrow, size)], ybuf.at[slot_, pl.ds(row, size)], sem.at[slot_])

    def chunks(g_, fn):
        _seg_chunks(seg_ref, np_ref, base_ref, anchor, g_, cap, fn)

    @pl.when(s == 0)
    def _():
        ybuf[...] = jnp.zeros_like(ybuf)
        chunks(g, lambda r, hr, n: copy(0, r, hr, n).start())

    @pl.when(s + 1 < nsteps)
    def _():
        chunks(g + 1, lambda r, hr, n: copy(1 - slot, r, hr, n).start())

    _wait_tile(seg_ref, np_ref, g, lambda n: copy(slot, 0, 0, n).wait())

    stacked = jnp.concatenate([pos_ref[...], jnp.zeros((8, TM), F32), rt_ref[...],
                               jnp.zeros((LANES - 32, TM), F32)], axis=0)
    cols = stacked.T
    pos_b = [jnp.broadcast_to(cols[:, k:k + 1], (TM, LANES)) for k in range(TOP_K)]
    wt_b = [jnp.broadcast_to(cols[:, 24 + k:25 + k], (TM, LANES)) for k in range(TOP_K)]
    npairs = (_tile_rows(seg_ref, np_ref, g) + (2 * CH - 1)) >> (CH_SHIFT + 1)
    acc[...] = jnp.zeros_like(acc)

    def chunk_pair(c2, carry):
        part = None
        for half in range(2):
            r0 = pl.multiple_of(c2 * (2 * CH) + half * CH, CH)
            w_parts = []
            for lane0 in range(0, CH, LANES):
                rio = (lax.broadcasted_iota(jnp.int32, (TM, LANES), 1) + (r0 + lane0)).astype(F32)
                w = jnp.zeros((TM, LANES), F32)
                for k in range(TOP_K):
                    w = jnp.where(rio == pos_b[k], wt_b[k], w)
                w_parts.append(w.astype(BF16))
            prod = _dot(jnp.concatenate(w_parts, axis=1), ybuf[slot, pl.ds(r0, CH), :])
            part = prod if part is None else part + prod
        acc[...] += part
        return carry
    lax.fori_loop(0, npairs, chunk_pair, 0)

    h = h2_ref[...].reshape(bb * tt, d)
    act = _silu(_dot(h, wsg_ref[...])) * _dot(h, wsu_ref[...])
    moe = acc[...] + _dot(act.astype(BF16), wsd_ref[...])
    x2 = x1_ref[...] + g2_ref[...] * moe.reshape(bb, tt, d)
    y_ref[...] = _rms(x2) * gf_ref[...]


def _combine(plan3, pos, rt, x1, h2, g2, wsg, wsu, wsd, gf, ys, bb, tt, t0, cap):
    b, t, d = x1.shape
    steps_t = t // tt
    nsteps = (b // bb) * steps_t
    bidx = lambda s: (s // steps_t, s % steps_t, 0)
    tok = pl.BlockSpec((bb, tt, d), lambda s, *_: bidx(s))
    full = lambda shape: pl.BlockSpec(shape, lambda s, *_: (0,) * len(shape))
    kern = functools.partial(_combine_kernel, t0=t0, nsteps=nsteps, cap=cap)
    return pl.pallas_call(
        kern,
        out_shape=jax.ShapeDtypeStruct((b, t, d), F32),
        grid_spec=pltpu.PrefetchScalarGridSpec(
            num_scalar_prefetch=3, grid=(nsteps,),
            in_specs=[pl.BlockSpec((8, TM), lambda s, *_: (0, s)), pl.BlockSpec((16, TM), lambda s, *_: (0, s)),
                      tok, tok, pl.BlockSpec((bb, 1, d), lambda s, *_: (s // steps_t, 0, 0)),
                      full(wsg.shape), full(wsu.shape), full(wsd.shape), full((1, 1, d)),
                      pl.BlockSpec(memory_space=pl.ANY)],
            out_specs=tok,
            scratch_shapes=[pltpu.VMEM((2, RMAX, d), BF16), pltpu.VMEM((TM, d), F32),
                            pltpu.VMEM((8, LANES), F32), pltpu.SemaphoreType.DMA((2,))]),
        compiler_params=_params("arbitrary"),
        name="combine",
    )(*plan3, pos, rt, x1, h2, g2, wsg, wsu, wsd, gf, ys)


def _plan(cnt, cap, max_tiles):
    np16 = (cnt + (SEG - 1)) // SEG
    npad = np16 * SEG
    seg = jnp.cumsum(npad, axis=1) - npad
    base = jnp.cumsum(npad, axis=0) - npad
    fin = jnp.sum(npad, axis=0)
    rem16 = ((-fin) % TR) // SEG
    tiles_e = (fin + (TR - 1)) // TR
    ends = jnp.cumsum(tiles_e)
    i = jnp.arange(max_tiles, dtype=jnp.int32)
    valid = i < ends[-1]
    ii = jnp.where(valid, i, jnp.maximum(ends[-1] - 1, 0))
    te = jnp.sum((ends[None, :] <= ii[:, None]).astype(jnp.int32), axis=1)
    te = jnp.minimum(te, N_EXPERTS - 1)
    owner = jnp.arange(N_EXPERTS, dtype=jnp.int32)[None, :] == te[:, None]
    r = ii - jnp.sum(jnp.where(owner, (ends - tiles_e)[None, :], 0), axis=1)
    tb = te * (cap // TR) + r
    first = jnp.logical_and(valid, r == 0)
    i32 = lambda a: a.astype(jnp.int32).reshape(-1)
    return (i32(seg), i32(np16), i32(base), i32(fin), i32(rem16)), (te, i32(tb), i32(valid), i32(first))


def kernel(x_prompt, x_sample, cache_a_k, cache_a_v, cache_b_k, cache_b_v, c_prompt, c_sample, w_ada, b_ada, w_in, lam_params, g_sub_a, g_sub_b, w_out, w_router, b_router, w_gate, w_up, w_down, w_shared_gate, w_shared_up, w_shared_down, g_final):
    bp, tp, d = x_prompt.shape
    bs, ts, _ = x_sample.shape
    past = cache_a_k.shape[2]

    mod = _ada(jnp.concatenate([c_prompt, c_sample], axis=0), w_ada[0], b_ada[0])
    mod = mod.reshape(bp + bs, 6, 1, d)
    sh1, sc1, g1, sh2, sc2, g2 = (mod[:, k] for k in range(6))

    w_in16 = w_in[0].astype(BF16)
    w_out16 = w_out[0].astype(BF16)
    wr_t = w_router[0].T
    br_col = b_router[0].reshape(N_EXPERTS, 1)
    wsg16, wsu16, wsd16 = (w_shared_gate[0].astype(BF16), w_shared_up[0].astype(BF16),
                           w_shared_down[0].astype(BF16))
    ga = g_sub_a[0].reshape(1, LANES)
    gb2 = jnp.concatenate([g_sub_b[0], g_sub_b[0]]).reshape(1, LANES)
    gf = g_final.reshape(1, 1, d)
    lam_p = lam_params[0]

    def front(x, sl, bb, tt, attn):
        q, ka, va, kb, vb, kv = _inproj(x, sc1[sl], sh1[sl], w_in16, bb, tt)
        o = attn(q, kv)
        x1, h2, rt, cnt = _outproj(o, x, g1[sl], sc2[sl], sh2[sl], w_out16, wr_t, br_col, bb, tt)
        return (x1, h2, rt, cnt), (ka, va, kb, vb)

    mid_p, rows_p = front(x_prompt, slice(0, bp), 1, 512,
                          lambda q, kv: _attn_prompt(lam_p, ga, gb2, q, kv, 512))
    caches = (cache_a_k.reshape(bs, past * A_HEADS, LANES), cache_a_v.reshape(bs, past * A_HEADS, LANES),
              jnp.transpose(cache_b_k[0], (0, 2, 3, 1)), jnp.transpose(cache_b_v[0], (0, 2, 3, 1)))
    mid_s, rows_s = front(x_sample, slice(bp, bp + bs), 8, ts,
                          lambda q, kv: _attn_sample(lam_p, ga, gb2, q, kv, *caches))

    (x1_p, h2_p, rt_p, cnt_p), (x1_s, h2_s, rt_s, cnt_s) = mid_p, mid_s
    n_p, n_s = bp * tp, bs * ts
    tiles_p, tiles_s = n_p // TM, n_s // TM
    tiles = tiles_p + tiles_s
    cap = -(-(n_p + n_s + tiles * (SEG - 1)) // TR) * TR
    max_tiles = (TOP_K * (n_p + n_s) + tiles * N_EXPERTS * (SEG - 1)) // TR + N_EXPERTS
    cnt = jnp.concatenate([cnt_p[:, 0, :], cnt_s[:, 0, :]], axis=0).astype(jnp.int32)
    plan, sched = _plan(cnt, cap, max_tiles)
    xs, pos_p = _dispatch(plan, rt_p, h2_p.reshape(n_p, d), None, 0, cap, False)
    xs, pos_s = _dispatch(plan, rt_s, h2_s.reshape(n_s, d), xs, tiles_p, cap, True)
    ys = _experts(sched, xs, w_gate[0], w_up[0], w_down[0], max_tiles)
    y_p = _combine(plan[:3], pos_p, rt_p, x1_p, h2_p, g2[0:bp], wsg16, wsu16, wsd16, gf, ys,
                   1, TM, 0, cap)
    y_s = _combine(plan[:3], pos_s, rt_s, x1_s, h2_s, g2[bp:bp + bs], wsg16, wsu16, wsd16, gf, ys,
                   TM // ts, ts, tiles_p, cap)

    def shape_rows(rows, b, t):
        ka, va, kb, vb = rows
        return (ka.reshape(1, b, t, A_HEADS, 2 * A_QK_DIM), va.reshape(1, b, t, A_HEADS, 2 * A_QK_DIM),
                kb.reshape(1, b, t, B_HEADS, B_DIM), vb.reshape(1, b, t, B_HEADS, B_DIM))

    return (y_p, y_s) + shape_rows(rows_p, bp, tp) + shape_rows(rows_s, bs, ts)
```

```python
import functools
import math

import jax
import jax.numpy as jnp
from jax import lax
from jax.experimental import pallas as pl
from jax.experimental.pallas import tpu as pltpu

F32 = jnp.float32
BF16 = jnp.bfloat16

D_MODEL = 1024
CHUNK = 64
A_QK_DIM = 64
A_HEADS = 4
A_WIDTH = 512
B_DIM = 64
B_HEADS = 8
B_WIDTH = 512
IN_COLS = 3 * A_WIDTH + 3 * B_WIDTH
N_EXPERTS = 64
TOP_K = 6
N_GROUPS = 8
TOPK_GROUPS = 4
D_EXPERT = 256
ROUTED_SCALE = 2.5
EPS = 1e-6
NEG_INF = -1e30
LANES = 128
QK_SCALE = A_QK_DIM ** -0.5
LAM_INIT = 0.8 - 0.6 * math.exp(-0.3 * 0)
LOG2E = math.log2(math.e)
SLOPES_LOG2 = tuple(LOG2E * 2.0 ** (-8.0 * (h + 1.0) / A_HEADS) for h in range(A_HEADS))
KV_KA, KV_VA, KV_KB, KV_VB = 0, 512, 1024, 1536
VMEM_LIMIT = 56 * 1024 * 1024
TM = 256
SEG_SHIFT = 4
SEG = 1 << SEG_SHIFT
SEG_BITS = 5
CH_SHIFT = 8
CH = 1 << CH_SHIFT
RMAX = 2560
TILE_BITS = 8
TR = 1152
FILL_BITS = 7

_NT = (((1,), (1,)), ((), ()))


def _dot(a, b):
    return jnp.dot(a, b, preferred_element_type=F32)


def _dot_nt(a, b):
    return lax.dot_general(a, b, _NT, preferred_element_type=F32)


def _rms(x):
    return x * lax.rsqrt(jnp.mean(x * x, axis=-1, keepdims=True) + EPS)


def _silu(x):
    return x * jax.nn.sigmoid(x)


def _params(*sem):
    return pltpu.CompilerParams(dimension_semantics=sem, vmem_limit_bytes=VMEM_LIMIT)


def _ada_kernel(c_ref, w_ref, b_ref, o_ref):
    s = _silu(c_ref[...]).astype(BF16)
    o_ref[...] = _dot(s, w_ref[...].astype(BF16)) + b_ref[...]


def _ada(c, w, b):
    n, d = c.shape
    cols = w.shape[1]
    tn = 1536
    return pl.pallas_call(
        _ada_kernel,
        out_shape=jax.ShapeDtypeStruct((n, cols), F32),
        grid=(cols // tn,),
        in_specs=[pl.BlockSpec((n, d), lambda j: (0, 0)),
                  pl.BlockSpec((d, tn), lambda j: (0, j)),
                  pl.BlockSpec((1, tn), lambda j: (0, j))],
        out_specs=pl.BlockSpec((n, tn), lambda j: (0, j)),
        compiler_params=_params("arbitrary"),
        name="ada",
    )(c, w, b.reshape(1, cols))


def _inproj_kernel(x_ref, sc_ref, sh_ref, w_ref, q_ref, ka_ref, va_ref, kb_ref, vb_ref, kv_ref):
    bb, tt, d = x_ref.shape
    h = _rms(x_ref[...]) * (1.0 + sc_ref[...]) + sh_ref[...]
    proj = _dot(h.reshape(bb * tt, d).astype(BF16), w_ref[...])

    def part(lo, width):
        return proj[:, lo:lo + width].reshape(bb, tt, width)

    qa, ka, va = part(0, 512), part(512, 512), part(1024, 512)
    qb, kb, vb = part(1536, 512), part(2048, 512), part(2560, 512)
    q_ref[:, :, 0:512] = (qa * (QK_SCALE * LOG2E)).astype(BF16)
    q_ref[:, :, 512:1024] = (qb * (QK_SCALE * LOG2E)).astype(BF16)
    for hd in range(A_HEADS):
        ka_ref[:, pl.ds(hd, tt, stride=A_HEADS), :] = ka[:, :, hd * LANES:(hd + 1) * LANES]
        va_ref[:, pl.ds(hd, tt, stride=A_HEADS), :] = va[:, :, hd * LANES:(hd + 1) * LANES]
    kb_ref[...] = kb
    vb_ref[...] = vb
    kv_ref[:, :, KV_KA:KV_KA + 512] = ka.astype(BF16)
    kv_ref[:, :, KV_VA:KV_VA + 512] = va.astype(BF16)
    kv_ref[:, :, KV_KB:KV_KB + 512] = kb.astype(BF16)
    kv_ref[:, :, KV_VB:KV_VB + 512] = vb.astype(BF16)


def _inproj(x, sc, sh, w_bf16, bb, tt):
    b, t, d = x.shape
    xspec = pl.BlockSpec((bb, tt, d), lambda i, j: (i, j, 0))
    mspec = pl.BlockSpec((bb, 1, d), lambda i, j: (i, 0, 0))

    def ospec(width):
        return pl.BlockSpec((bb, tt, width), lambda i, j: (i, j, 0))

    f32rows = jax.ShapeDtypeStruct((b, t, 512), F32)
    a_rows = jax.ShapeDtypeStruct((b, t * A_HEADS, LANES), F32)
    a_spec = pl.BlockSpec((bb, tt * A_HEADS, LANES), lambda i, j: (i, j, 0))
    return pl.pallas_call(
        _inproj_kernel,
        out_shape=(jax.ShapeDtypeStruct((b, t, 1024), BF16), a_rows, a_rows, f32rows, f32rows,
                   jax.ShapeDtypeStruct((b, t, 2048), BF16)),
        grid=(b // bb, t // tt),
        in_specs=[xspec, mspec, mspec, pl.BlockSpec((d, IN_COLS), lambda i, j: (0, 0))],
        out_specs=(ospec(1024), a_spec, a_spec, ospec(512), ospec(512), ospec(2048)),
        compiler_params=_params("arbitrary", "arbitrary"),
        name="inproj",
    )(x, sc, sh, w_bf16)


def _lane_lo(shape):
    return lax.broadcasted_iota(jnp.int32, shape, len(shape) - 1) < 64


def _split_halves(x):
    lo = _lane_lo(x.shape)
    zero = jnp.zeros_like(x)
    return jnp.concatenate([jnp.where(lo, x, zero), jnp.where(lo, zero, x)], axis=0)


def _lam_of(lam_ref):
    lp = lam_ref[...]
    s1 = jnp.sum(lp[0:1] * lp[1:2], axis=1, keepdims=True)
    s2 = jnp.sum(lp[2:3] * lp[3:4], axis=1, keepdims=True)
    return jnp.exp(s1) - jnp.exp(s2) + LAM_INIT


def _a_update(state, s, v16):
    m, l, acc = state
    m_new = jnp.maximum(m, jnp.max(s, axis=1, keepdims=True))
    alpha = jnp.exp2(m - m_new)
    p = jnp.exp2(s - m_new)
    l = alpha * l + jnp.sum(p, axis=1, keepdims=True)
    acc = alpha * acc + _dot(p.astype(BF16), v16)
    return m_new, l, acc


def _a_init(rows):
    return (jnp.full((rows, 1), NEG_INF, F32), jnp.zeros((rows, 1), F32), jnp.zeros((rows, LANES), F32))


def _a_finish(state, lam, gain, nq):
    _, l, acc = state
    o = acc[:nq] / l[:nq] - lam * (acc[nq:] / l[nq:])
    return _rms(o) * gain * (1.0 - LAM_INIT)


def _diag_tiles(nq, nk):
    r = lax.broadcasted_iota(jnp.int32, (2 * nq, nk), 0)
    r = jnp.where(r >= nq, r - nq, r)
    c = lax.broadcasted_iota(jnp.int32, (2 * nq, nk), 1)
    visible = (c // CHUNK) <= (r // CHUNK)
    before = c < r
    dist_shift = (r - jnp.abs(r - c)).astype(F32)
    return visible, before, dist_shift


def _strict_upper(n):
    j = lax.broadcasted_iota(jnp.int32, (n, n), 0)
    s = lax.broadcasted_iota(jnp.int32, (n, n), 1)
    return jnp.where(j > s, 1.0, 0.0).astype(BF16)


def _b_block(z, v16, tri, carry, acc, before, nq, v_transposed=False):
    sp = jnp.where(z > 60.0, z, jnp.log2(1.0 + jnp.exp2(z)))
    if before is not None:
        sp = jnp.where(before, sp, 0.0)
    tk = tri.shape[0]
    sp16 = sp.astype(BF16)
    pieces, right = [], None
    for c in reversed(range(z.shape[1] // tk)):
        piece = _dot(sp16[:, c * tk:(c + 1) * tk], tri)
        pieces.append(piece if right is None else piece + right)
        chunk_sum = jnp.sum(sp[:, c * tk:(c + 1) * tk], axis=1, keepdims=True)
        right = chunk_sum if right is None else right + chunk_sum
    later = pieces[0] if len(pieces) == 1 else jnp.concatenate(pieces[::-1], axis=1)
    a = jnp.exp2((z - sp) - (carry + later))
    if before is not None:
        a = jnp.where(before, a, 0.0)
    a16 = a.astype(BF16)
    vzero = jnp.zeros_like(v16)
    if v_transposed:
        first = lax.broadcasted_iota(jnp.int32, v16.shape, 0) < B_DIM
        acc = acc + _dot_nt(a16[:nq], jnp.where(first, v16, vzero)) + _dot_nt(a16[nq:], jnp.where(first, vzero, v16))
    else:
        vlo = _lane_lo(v16.shape)
        acc = acc + _dot(a16[:nq], jnp.where(vlo, v16, vzero)) + _dot(a16[nq:], jnp.where(vlo, vzero, v16))
    return carry + right, acc


def _b_finish(acc, gain2):
    lo = _lane_lo(acc.shape)
    ss = acc * acc
    s_lo = jnp.sum(jnp.where(lo, ss, 0.0), axis=1, keepdims=True)
    s_hi = jnp.sum(jnp.where(lo, 0.0, ss), axis=1, keepdims=True)
    ms = jnp.where(lo, s_lo, s_hi) * (1.0 / B_DIM)
    return acc * lax.rsqrt(ms + EPS) * gain2


def _attn_prompt_kernel(lam_ref, ga_ref, gb_ref, q_ref, kv_ref, o_ref):
    bq = q_ref.shape[1]
    pairs = B_HEADS // 2
    i = pl.program_id(1)
    q0 = pl.multiple_of(i * bq, bq)
    lam = _lam_of(lam_ref)
    visible, before, dist_shift = _diag_tiles(bq, bq)
    col = lax.broadcasted_iota(jnp.int32, (1, bq), 1)
    tri = _strict_upper(min(bq, 256))
    qa = [_split_halves(q_ref[0, :, h * LANES:(h + 1) * LANES]) for h in range(A_HEADS)]
    qb = [_split_halves(q_ref[0, :, A_WIDTH + p * LANES:A_WIDTH + (p + 1) * LANES]) for p in range(pairs)]

    def kv(rows, base, n):
        return kv_ref[0, rows, base + n * LANES:base + (n + 1) * LANES]

    rows = pl.ds(q0, bq)
    a_st = tuple(
        _a_update(_a_init(2 * bq),
                  jnp.where(visible, _dot_nt(qa[h], kv(rows, KV_KA, h)) + SLOPES_LOG2[h] * dist_shift, NEG_INF),
                  kv(rows, KV_VA, h))
        for h in range(A_HEADS))
    b_st = tuple(
        _b_block(_dot_nt(qb[p], kv(rows, KV_KB, p)), kv(rows, KV_VB, p), tri,
                 jnp.zeros((2 * bq, 1), F32), jnp.zeros((bq, LANES), F32), before, bq)
        for p in range(pairs))

    def body(jj, st):
        a_st, b_st = st
        k0 = pl.multiple_of((i - 1 - jj) * bq, bq)
        rows = pl.ds(k0, bq)
        rel = (col + (k0 - q0)).astype(F32)
        a_new = tuple(
            _a_update(a_st[h], _dot_nt(qa[h], kv(rows, KV_KA, h)) + SLOPES_LOG2[h] * rel, kv(rows, KV_VA, h))
            for h in range(A_HEADS))
        b_new = tuple(
            _b_block(_dot_nt(qb[p], kv(rows, KV_KB, p)), kv(rows, KV_VB, p), tri, b_st[p][0], b_st[p][1], None, bq)
            for p in range(pairs))
        return a_new, b_new

    a_st, b_st = lax.fori_loop(0, i, body, (a_st, b_st))
    for h in range(A_HEADS):
        o_ref[0, :, h * LANES:(h + 1) * LANES] = _a_finish(a_st[h], lam, ga_ref[...], bq).astype(o_ref.dtype)
    for p in range(pairs):
        o_ref[0, :, A_WIDTH + p * LANES:A_WIDTH + (p + 1) * LANES] = (
            _b_finish(b_st[p][1], gb_ref[...]).astype(o_ref.dtype))


def _attn_prompt(lam_p, ga, gb2, q, kv, bq):
    b, t, _ = q.shape
    small = lambda shape: pl.BlockSpec(shape, lambda bi, i: (0, 0))
    return pl.pallas_call(
        _attn_prompt_kernel,
        out_shape=jax.ShapeDtypeStruct((b, t, D_MODEL), BF16),
        grid=(b, t // bq),
        in_specs=[small((4, A_QK_DIM)), small((1, LANES)), small((1, LANES)),
                  pl.BlockSpec((1, bq, D_MODEL), lambda bi, i: (bi, i, 0)),
                  pl.BlockSpec((1, t, 2048), lambda bi, i: (bi, 0, 0), pipeline_mode=pl.Buffered(1))],
        out_specs=pl.BlockSpec((1, bq, D_MODEL), lambda bi, i: (bi, i, 0)),
        compiler_params=_params("arbitrary", "arbitrary"),
        name="attn_prompt",
    )(lam_p, ga, gb2, q, kv)


def _attn_sample_kernel(lam_ref, ga_ref, gb_ref, q_ref, kv_ref, cak_ref, cav_ref, cbk_ref, cbv_ref, o_ref):
    nq = q_ref.shape[1]
    past = cak_ref.shape[1] // A_HEADS
    pb = 256
    lam = _lam_of(lam_ref)
    visible, before, dist_shift = _diag_tiles(nq, nq)
    pcol = lax.broadcasted_iota(jnp.int32, (1, past), 1)
    tri_new = _strict_upper(nq)
    tri_past = _strict_upper(pb)

    for h in range(A_HEADS):
        cs = slice(h * LANES, (h + 1) * LANES)
        qs = _split_halves(q_ref[0, :, cs])
        slope = SLOPES_LOG2[h]
        head_rows = pl.ds(h, past, stride=A_HEADS)
        s_past = _dot_nt(qs, cak_ref[0, head_rows, :].astype(BF16)) + slope * (pcol - past).astype(F32)
        state = _a_update(_a_init(2 * nq), s_past, cav_ref[0, head_rows, :].astype(BF16))
        k16 = kv_ref[0, :, KV_KA + h * LANES:KV_KA + (h + 1) * LANES]
        v16 = kv_ref[0, :, KV_VA + h * LANES:KV_VA + (h + 1) * LANES]
        s_new = jnp.where(visible, _dot_nt(qs, k16) + slope * dist_shift, NEG_INF)
        state = _a_update(state, s_new, v16)
        o_ref[0, :, cs] = _a_finish(state, lam, ga_ref[...], nq).astype(o_ref.dtype)

    for p in range(B_HEADS // 2):
        cs = slice(p * LANES, (p + 1) * LANES)
        qs = _split_halves(q_ref[0, :, A_WIDTH + p * LANES:A_WIDTH + (p + 1) * LANES])
        k16 = kv_ref[0, :, KV_KB + p * LANES:KV_KB + (p + 1) * LANES]
        v16 = kv_ref[0, :, KV_VB + p * LANES:KV_VB + (p + 1) * LANES]
        carry, acc = _b_block(_dot_nt(qs, k16), v16, tri_new,
                              jnp.zeros((2 * nq, 1), F32), jnp.zeros((nq, LANES), F32), before, nq)
        for j in reversed(range(past // pb)):
            keys = slice(j * pb, (j + 1) * pb)
            k_t = cbk_ref[0, 2 * p:2 * p + 2, :, keys].reshape(LANES, pb).astype(BF16)
            v_t = cbv_ref[0, 2 * p:2 * p + 2, :, keys].reshape(LANES, pb).astype(BF16)
            carry, acc = _b_block(_dot(qs, k_t), v_t, tri_past, carry, acc, None, nq, v_transposed=True)
        o_ref[0, :, A_WIDTH + p * LANES:A_WIDTH + (p + 1) * LANES] = _b_finish(acc, gb_ref[...]).astype(o_ref.dtype)


def _attn_sample(lam_p, ga, gb2, q, kv, cak, cav, cbk, cbv):
    b, t, _ = q.shape
    past = cak.shape[1] // A_HEADS
    small = lambda shape: pl.BlockSpec(shape, lambda bi: (0, 0))
    per_b = lambda rows, width: pl.BlockSpec((1, rows, width), lambda bi: (bi, 0, 0))
    a_cache = per_b(past * A_HEADS, LANES)
    b_cache = pl.BlockSpec((1, B_HEADS, B_DIM, past), lambda bi: (bi, 0, 0, 0))
    return pl.pallas_call(
        _attn_sample_kernel,
        out_shape=jax.ShapeDtypeStruct((b, t, D_MODEL), BF16),
        grid=(b,),
        in_specs=[small((4, A_QK_DIM)), small((1, LANES)), small((1, LANES)),
                  per_b(t, D_MODEL), per_b(t, 2048), a_cache, a_cache, b_cache, b_cache],
        out_specs=per_b(t, D_MODEL),
        compiler_params=_params("arbitrary"),
        name="attn_sample",
    )(lam_p, ga, gb2, q, kv, cak, cav, cbk, cbv)


def _route(logits_t, bias_col):
    n = logits_t.shape[1]
    scores = jax.nn.sigmoid(logits_t)
    sel3 = (scores + bias_col).reshape(N_GROUPS, 8, n)
    eio = lax.broadcasted_iota(jnp.int32, (N_GROUPS, 8, n), 1)
    m1 = jnp.max(sel3, axis=1, keepdims=True)
    i1 = jnp.min(jnp.where(sel3 == m1, eio, 8), axis=1, keepdims=True)
    m2 = jnp.max(jnp.where(eio == i1, -jnp.inf, sel3), axis=1, keepdims=True)
    gscore = (m1 + m2).reshape(N_GROUPS, n)
    gio = lax.broadcasted_iota(jnp.int32, (N_GROUPS, n), 0)
    keep = jnp.zeros((N_GROUPS, n), F32)
    for _ in range(TOPK_GROUPS):
        g = jnp.max(gscore, axis=0, keepdims=True)
        gi = jnp.min(jnp.where(gscore == g, gio, N_GROUPS), axis=0, keepdims=True)
        hit = gio == gi
        keep = jnp.where(hit, 1.0, keep)
        gscore = jnp.where(hit, -jnp.inf, gscore)
    cur = jnp.where(keep.reshape(N_GROUPS, 1, n) > 0.0, sel3, NEG_INF).reshape(N_EXPERTS, n)
    xio = lax.broadcasted_iota(jnp.int32, (N_EXPERTS, n), 0)
    chosen = jnp.zeros((N_EXPERTS, n), F32)
    ids, raw = [], []
    for _ in range(TOP_K):
        mx = jnp.max(cur, axis=0, keepdims=True)
        ei = jnp.min(jnp.where(cur == mx, xio, N_EXPERTS), axis=0, keepdims=True)
        hit = xio == ei
        chosen = jnp.where(hit, 1.0, chosen)
        cur = jnp.where(hit, -jnp.inf, cur)
        ids.append(ei.astype(F32))
        raw.append(jnp.sum(jnp.where(hit, scores, 0.0), axis=0, keepdims=True))
    denom = raw[0]
    for r in raw[1:]:
        denom = denom + r
    wts = [r / denom * ROUTED_SCALE for r in raw]
    pad = jnp.zeros((8 - TOP_K, n), F32)
    return jnp.concatenate(ids + [pad] + wts + [pad], axis=0), chosen


def _outproj_kernel(o_ref, x_ref, g1_ref, sc_ref, sh_ref, wout_ref, wrt_ref, br_ref, x1_ref, h2_ref, rt_ref, cnt_ref):
    bb, tt, d = x_ref.shape
    n = bb * tt
    proj = _dot(o_ref[...].reshape(n, d), wout_ref[...]).reshape(bb, tt, d)
    x1 = x_ref[...] + g1_ref[...] * proj
    x1_ref[...] = x1
    h2 = (_rms(x1) * (1.0 + sc_ref[...]) + sh_ref[...]).reshape(n, d)
    h2hi = h2.astype(BF16)
    h2_ref[...] = h2hi.reshape(bb, tt, d)
    h2lo = (h2 - h2hi.astype(F32)).astype(BF16)
    w = wrt_ref[...]
    whi = w.astype(BF16)
    wlo = (w - whi.astype(F32)).astype(BF16)
    logits_t = _dot_nt(whi, h2hi) + _dot_nt(whi, h2lo) + _dot_nt(wlo, h2hi)
    rt, chosen = _route(logits_t, br_ref[...])
    rt_ref[...] = rt
    ones = jnp.ones((8, TM), BF16)
    for j in range(n // TM):
        cnt_ref[j] = _dot_nt(ones, chosen[:, j * TM:(j + 1) * TM].astype(BF16))


def _outproj(o, x, g1, sc, sh, wout_bf16, wr_t, br_col, bb, tt):
    b, t, d = x.shape
    n = bb * tt
    steps_t = t // tt
    tok = pl.BlockSpec((bb, tt, d), lambda i, j: (i, j, 0))
    mspec = pl.BlockSpec((bb, 1, d), lambda i, j: (i, 0, 0))
    full = lambda shape: pl.BlockSpec(shape, lambda i, j: (0, 0))
    return pl.pallas_call(
        _outproj_kernel,
        out_shape=(jax.ShapeDtypeStruct((b, t, d), F32), jax.ShapeDtypeStruct((b, t, d), BF16),
                   jax.ShapeDtypeStruct((16, b * t), F32),
                   jax.ShapeDtypeStruct((b * t // TM, 8, N_EXPERTS), F32)),
        grid=(b // bb, steps_t),
        in_specs=[tok, tok, mspec, mspec, mspec, full((d, d)), full((N_EXPERTS, d)), full((N_EXPERTS, 1))],
        out_specs=(tok, tok,
                   pl.BlockSpec((16, n), lambda i, j: (0, i * steps_t + j)),
                   pl.BlockSpec((n // TM, 8, N_EXPERTS), lambda i, j: (i * steps_t + j, 0, 0))),
        compiler_params=_params("arbitrary", "arbitrary"),
        name="outproj_router",
    )(o, x, g1, sc, sh, wout_bf16, wr_t, br_col)


FAST_BIT = 1
EXPERT_UNROLL = 4


def _seg_chunks(seg_ref, np_ref, base_ref, anchor, g, cap, visit):
    def body(e4, c):
        def chunk(seg, b):
            s0, n16, b0 = seg
            off = (n16 & ((1 << b) - 1)) * SEG

            @pl.when(((n16 >> b) & 1) == 1)
            def _():
                visit(pl.multiple_of(s0 + off, SEG), pl.multiple_of(b0 + off, SEG), SEG << b)

        segs = []
        for u in range(EXPERT_UNROLL):
            e = e4 * EXPERT_UNROLL + u
            idx = g * N_EXPERTS + e
            segs.append((seg_ref[idx], np_ref[idx], e * cap + base_ref[idx]))
        for s0, n16, b0 in segs:
            @pl.when(n16 == (1 << FAST_BIT))
            def _(s0=s0, b0=b0):
                visit(pl.multiple_of(s0, SEG), pl.multiple_of(b0, SEG), SEG << FAST_BIT)
        for seg in segs:
            @pl.when(jnp.logical_and(seg[1] != (1 << FAST_BIT), seg[1] != 0))
            def _(seg=seg):
                anchor[...] = jnp.zeros_like(anchor)
                for b in range(SEG_BITS):
                    chunk(seg, b)
        return c
    lax.fori_loop(0, N_EXPERTS // EXPERT_UNROLL, body, 0)


def _tile_rows(seg_ref, np_ref, g):
    last = g * N_EXPERTS + N_EXPERTS - 1
    return seg_ref[last] + np_ref[last] * SEG


def _wait_tile(seg_ref, np_ref, g, wait_rows):
    n16 = _tile_rows(seg_ref, np_ref, g) >> SEG_SHIFT
    for b in range(TILE_BITS):
        @pl.when(((n16 >> b) & 1) == 1)
        def _(b=b):
            wait_rows(SEG << b)


def _dispatch_kernel(seg_ref, np_ref, base_ref, fin_ref, rem_ref, rt_ref, h2_ref, *rest, t0, nsteps, cap, fill):
    xs_ref, pos_ref, buf, zbuf, anchor, sem, zsem = rest[-7:]
    s = pl.program_id(0)
    g = t0 + s
    slot = s % 2

    def copy(slot_, row, hrow, size):
        return pltpu.make_async_copy(buf.at[slot_, pl.ds(row, size)], xs_ref.at[pl.ds(hrow, size)], sem.at[slot_])

    def chunks(g_, fn):
        _seg_chunks(seg_ref, np_ref, base_ref, anchor, g_, cap, fn)

    def wait_tile(g_, slot_):
        _wait_tile(seg_ref, np_ref, g_, lambda n: copy(slot_, 0, 0, n).wait())

    @pl.when(s >= 2)
    def _():
        wait_tile(g - 2, slot)

    rt = rt_ref[...]
    eio = lax.broadcasted_iota(jnp.int32, (N_EXPERTS, TM), 0).astype(F32)
    chosen = jnp.zeros((N_EXPERTS, TM), F32)
    for k in range(TOP_K):
        chosen = jnp.where(eio == rt[k:k + 1], 1.0, chosen)
    t_row = lax.broadcasted_iota(jnp.int32, (TM, TM), 0)
    t_col = lax.broadcasted_iota(jnp.int32, (TM, TM), 1)
    rank = _dot(chosen.astype(BF16), jnp.where(t_row < t_col, 1.0, 0.0).astype(BF16))
    n_e = jnp.sum(chosen, axis=1, keepdims=True).astype(jnp.int32)
    np16 = ((n_e + (SEG - 1)) >> SEG_SHIFT).astype(F32)
    e_row = lax.broadcasted_iota(jnp.int32, (N_EXPERTS, N_EXPERTS), 0)
    e_col = lax.broadcasted_iota(jnp.int32, (N_EXPERTS, N_EXPERTS), 1)
    lower = jnp.where(e_col < e_row, 1.0, 0.0).astype(BF16)
    seg0 = SEG * _dot(lower, jnp.broadcast_to(np16, (N_EXPERTS, LANES)).astype(BF16))[:, 0:1]
    pos = seg0 + rank
    pos_rows = [jnp.sum(jnp.where(eio == rt[k:k + 1], pos, 0.0), axis=0, keepdims=True) for k in range(TOP_K)]
    pos_ref[...] = jnp.concatenate(pos_rows + [jnp.zeros((8 - TOP_K, TM), F32)], axis=0)

    h2 = h2_ref[...]
    npairs = (_tile_rows(seg_ref, np_ref, g) + (2 * CH - 1)) >> (CH_SHIFT + 1)

    def chunk_pair(c2, carry):
        for half in range(2):
            r0 = pl.multiple_of(c2 * (2 * CH) + half * CH, CH)
            rio = (lax.broadcasted_iota(jnp.int32, (CH, TM), 0) + r0).astype(F32)
            p = jnp.zeros((CH, TM), F32)
            for k in range(TOP_K):
                p = jnp.where(rio == pos_rows[k], 1.0, p)
            buf[slot, pl.ds(r0, CH), :] = _dot(p.astype(BF16), h2).astype(BF16)
        return carry
    lax.fori_loop(0, npairs, chunk_pair, 0)

    chunks(g, lambda r, hr, n: copy(slot, r, hr, n).start())

    @pl.when(s == nsteps - 1)
    def _():
        if nsteps >= 2:
            wait_tile(g - 1, 1 - slot)
        wait_tile(g, slot)
        if fill:
            zbuf[...] = jnp.zeros_like(zbuf)

            def zero_chunks(fn):
                def body(e, c):
                    r16 = rem_ref[e]
                    h0 = e * cap + fin_ref[e]
                    for b in range(FILL_BITS):
                        off = (r16 & ((1 << b) - 1)) * SEG

                        @pl.when(((r16 >> b) & 1) == 1)
                        def _(b=b, off=off):
                            size = SEG << b
                            fn(pltpu.make_async_copy(zbuf.at[pl.ds(0, size)],
                                                     xs_ref.at[pl.ds(pl.multiple_of(h0 + off, SEG), size)], zsem))
                    return c
                lax.fori_loop(0, N_EXPERTS, body, 0)
            zero_chunks(lambda cp: cp.start())
            zero_chunks(lambda cp: cp.wait())


def _dispatch(plan, rt, h2, xs, t0, cap, fill):
    n, d = h2.shape
    nsteps = n // TM
    kern = functools.partial(_dispatch_kernel, t0=t0, nsteps=nsteps, cap=cap, fill=fill)
    in_specs = [pl.BlockSpec((16, TM), lambda s, *_: (0, s)), pl.BlockSpec((TM, d), lambda s, *_: (s, 0))]
    args = [rt, h2]
    aliases = {}
    if xs is not None:
        in_specs.append(pl.BlockSpec(memory_space=pl.ANY))
        args.append(xs)
        aliases = {len(plan) + 2: 0}
    return pl.pallas_call(
        kern,
        out_shape=(jax.ShapeDtypeStruct((N_EXPERTS * cap, d), BF16), jax.ShapeDtypeStruct((8, n), F32)),
        grid_spec=pltpu.PrefetchScalarGridSpec(
            num_scalar_prefetch=len(plan), grid=(nsteps,), in_specs=in_specs,
            out_specs=(pl.BlockSpec(memory_space=pl.ANY), pl.BlockSpec((8, TM), lambda s, *_: (0, s))),
            scratch_shapes=[pltpu.VMEM((2, RMAX, d), BF16), pltpu.VMEM((SEG << (FILL_BITS - 1), d), BF16),
                            pltpu.VMEM((8, LANES), F32),
                            pltpu.SemaphoreType.DMA((2,)), pltpu.SemaphoreType.DMA(())]),
        input_output_aliases=aliases,
        compiler_params=_params("arbitrary"),
        name="dispatch",
    )(*plan, *args)


def _expert_kernel(te_ref, tb_ref, tv_ref, tf_ref, xs_ref, wg_ref, wu_ref, wd_ref, ys_ref, wg16, wu16, wd16):
    i = pl.program_id(0)

    @pl.when(tf_ref[i] == 1)
    def _():
        wg16[...] = wg_ref[0].astype(BF16)
        wu16[...] = wu_ref[0].astype(BF16)
        wd16[...] = wd_ref[0].astype(BF16)

    @pl.when(tv_ref[i] == 1)
    def _():
        x = xs_ref[...]
        act = _silu(_dot(x, wg16[...])) * _dot(x, wu16[...])
        ys_ref[...] = _dot(act.astype(BF16), wd16[...]).astype(BF16)


def _experts(sched, xs, wg, wu, wd, max_tiles):
    d = xs.shape[1]
    wspec = lambda shape: pl.BlockSpec((1,) + shape, lambda i, te, tb, tv, tf: (te[i], 0, 0))
    row_map = lambda i, te, tb, tv, tf: (tb[i], 0)
    return pl.pallas_call(
        _expert_kernel,
        out_shape=jax.ShapeDtypeStruct(xs.shape, BF16),
        grid_spec=pltpu.PrefetchScalarGridSpec(
            num_scalar_prefetch=4, grid=(max_tiles,),
            in_specs=[pl.BlockSpec((TR, d), row_map),
                      wspec((d, D_EXPERT)), wspec((d, D_EXPERT)), wspec((D_EXPERT, d))],
            out_specs=pl.BlockSpec((TR, d), row_map),
            scratch_shapes=[pltpu.VMEM((d, D_EXPERT), BF16), pltpu.VMEM((d, D_EXPERT), BF16),
                            pltpu.VMEM((D_EXPERT, d), BF16)]),
        compiler_params=_params("arbitrary"),
        name="experts",
    )(*sched, xs, wg, wu, wd)


def _combine_kernel(seg_ref, np_ref, base_ref, pos_ref, rt_ref, x1_ref, h2_ref, g2_ref, wsg_ref, wsu_ref, wsd_ref,
                    gf_ref, ys_ref, y_ref, ybuf, acc, anchor, sem, *, t0, nsteps, cap):
    bb, tt, d = x1_ref.shape
    s = pl.program_id(0)
    g = t0 + s
    slot = s % 2

    def copy(slot_, row, hrow, size):
        return pltpu.make_async_copy(ys_ref.at[pl.ds(hrow, size)], ybuf.at[slot_, pl.ds(row, size)], sem.at[slot_])

    def chunks(g_, fn):
        _seg_chunks(seg_ref, np_ref, base_ref, anchor, g_, cap, fn)

    @pl.when(s == 0)
    def _():
        ybuf[...] = jnp.zeros_like(ybuf)
        chunks(g, lambda r, hr, n: copy(0, r, hr, n).start())

    @pl.when(s + 1 < nsteps)
    def _():
        chunks(g + 1, lambda r, hr, n: copy(1 - slot, r, hr, n).start())

    _wait_tile(seg_ref, np_ref, g, lambda n: copy(slot, 0, 0, n).wait())

    stacked = jnp.concatenate([pos_ref[...], jnp.zeros((8, TM), F32), rt_ref[...],
                               jnp.zeros((LANES - 32, TM), F32)], axis=0)
    cols = stacked.T
    pos_b = [jnp.broadcast_to(cols[:, k:k + 1], (TM, LANES)) for k in range(TOP_K)]
    wt_b = [jnp.broadcast_to(cols[:, 24 + k:25 + k], (TM, LANES)) for k in range(TOP_K)]
    npairs = (_tile_rows(seg_ref, np_ref, g) + (2 * CH - 1)) >> (CH_SHIFT + 1)
    acc[...] = jnp.zeros_like(acc)

    def chunk_pair(c2, carry):
        part = None
        for half in range(2):
            r0 = pl.multiple_of(c2 * (2 * CH) + half * CH, CH)
            w_parts = []
            for lane0 in range(0, CH, LANES):
                rio = (lax.broadcasted_iota(jnp.int32, (TM, LANES), 1) + (r0 + lane0)).astype(F32)
                w = jnp.zeros((TM, LANES), F32)
                for k in range(TOP_K):
                    w = jnp.where(rio == pos_b[k], wt_b[k], w)
                w_parts.append(w.astype(BF16))
            prod = _dot(jnp.concatenate(w_parts, axis=1), ybuf[slot, pl.ds(r0, CH), :])
            part = prod if part is None else part + prod
        acc[...] += part
        return carry
    lax.fori_loop(0, npairs, chunk_pair, 0)

    h = h2_ref[...].reshape(bb * tt, d)
    act = _silu(_dot(h, wsg_ref[...])) * _dot(h, wsu_ref[...])
    moe = acc[...] + _dot(act.astype(BF16), wsd_ref[...])
    x2 = x1_ref[...] + g2_ref[...] * moe.reshape(bb, tt, d)
    y_ref[...] = _rms(x2) * gf_ref[...]


def _combine(plan3, pos, rt, x1, h2, g2, wsg, wsu, wsd, gf, ys, bb, tt, t0, cap):
    b, t, d = x1.shape
    steps_t = t // tt
    nsteps = (b // bb) * steps_t
    bidx = lambda s: (s // steps_t, s % steps_t, 0)
    tok = pl.BlockSpec((bb, tt, d), lambda s, *_: bidx(s))
    full = lambda shape: pl.BlockSpec(shape, lambda s, *_: (0,) * len(shape))
    kern = functools.partial(_combine_kernel, t0=t0, nsteps=nsteps, cap=cap)
    return pl.pallas_call(
        kern,
        out_shape=jax.ShapeDtypeStruct((b, t, d), F32),
        grid_spec=pltpu.PrefetchScalarGridSpec(
            num_scalar_prefetch=3, grid=(nsteps,),
            in_specs=[pl.BlockSpec((8, TM), lambda s, *_: (0, s)), pl.BlockSpec((16, TM), lambda s, *_: (0, s)),
                      tok, tok, pl.BlockSpec((bb, 1, d), lambda s, *_: (s // steps_t, 0, 0)),
                      full(wsg.shape), full(wsu.shape), full(wsd.shape), full((1, 1, d)),
                      pl.BlockSpec(memory_space=pl.ANY)],
            out_specs=tok,
            scratch_shapes=[pltpu.VMEM((2, RMAX, d), BF16), pltpu.VMEM((TM, d), F32),
                            pltpu.VMEM((8, LANES), F32), pltpu.SemaphoreType.DMA((2,))]),
        compiler_params=_params("arbitrary"),
        name="combine",
    )(*plan3, pos, rt, x1, h2, g2, wsg, wsu, wsd, gf, ys)


def _plan(cnt, cap, max_tiles):
    np16 = (cnt + (SEG - 1)) // SEG
    npad = np16 * SEG
    seg = jnp.cumsum(npad, axis=1) - npad
    base = jnp.cumsum(npad, axis=0) - npad
    fin = jnp.sum(npad, axis=0)
    rem16 = ((-fin) % TR) // SEG
    tiles_e = (fin + (TR - 1)) // TR
    ends = jnp.cumsum(tiles_e)
    i = jnp.arange(max_tiles, dtype=jnp.int32)
    valid = i < ends[-1]
    ii = jnp.where(valid, i, jnp.maximum(ends[-1] - 1, 0))
    te = jnp.sum((ends[None, :] <= ii[:, None]).astype(jnp.int32), axis=1)
    te = jnp.minimum(te, N_EXPERTS - 1)
    owner = jnp.arange(N_EXPERTS, dtype=jnp.int32)[None, :] == te[:, None]
    r = ii - jnp.sum(jnp.where(owner, (ends - tiles_e)[None, :], 0), axis=1)
    tb = te * (cap // TR) + r
    first = jnp.logical_and(valid, r == 0)
    i32 = lambda a: a.astype(jnp.int32).reshape(-1)
    return (i32(seg), i32(np16), i32(base), i32(fin), i32(rem16)), (te, i32(tb), i32(valid), i32(first))


def kernel(x_prompt, x_sample, cache_a_k, cache_a_v, cache_b_k, cache_b_v, c_prompt, c_sample, w_ada, b_ada, w_in, lam_params, g_sub_a, g_sub_b, w_out, w_router, b_router, w_gate, w_up, w_down, w_shared_gate, w_shared_up, w_shared_down, g_final):
    bp, tp, d = x_prompt.shape
    bs, ts, _ = x_sample.shape
    past = cache_a_k.shape[2]

    mod = _ada(jnp.concatenate([c_prompt, c_sample], axis=0), w_ada[0], b_ada[0])
    mod = mod.reshape(bp + bs, 6, 1, d)
    sh1, sc1, g1, sh2, sc2, g2 = (mod[:, k] for k in range(6))

    w_in16 = w_in[0].astype(BF16)
    w_out16 = w_out[0].astype(BF16)
    wr_t = w_router[0].T
    br_col = b_router[0].reshape(N_EXPERTS, 1)
    wsg16, wsu16, wsd16 = (w_shared_gate[0].astype(BF16), w_shared_up[0].astype(BF16),
                           w_shared_down[0].astype(BF16))
    ga = g_sub_a[0].reshape(1, LANES)
    gb2 = jnp.concatenate([g_sub_b[0], g_sub_b[0]]).reshape(1, LANES)
    gf = g_final.reshape(1, 1, d)
    lam_p = lam_params[0]

    def front(x, sl, bb, tt, attn):
        q, ka, va, kb, vb, kv = _inproj(x, sc1[sl], sh1[sl], w_in16, bb, tt)
        o = attn(q, kv)
        x1, h2, rt, cnt = _outproj(o, x, g1[sl], sc2[sl], sh2[sl], w_out16, wr_t, br_col, bb, tt)
        return (x1, h2, rt, cnt), (ka, va, kb, vb)

    mid_p, rows_p = front(x_prompt, slice(0, bp), 1, 512,
                          lambda q, kv: _attn_prompt(lam_p, ga, gb2, q, kv, 512))
    caches = (cache_a_k.reshape(bs, past * A_HEADS, LANES), cache_a_v.reshape(bs, past * A_HEADS, LANES),
              jnp.transpose(cache_b_k[0], (0, 2, 3, 1)), jnp.transpose(cache_b_v[0], (0, 2, 3, 1)))
    mid_s, rows_s = front(x_sample, slice(bp, bp + bs), 8, ts,
                          lambda q, kv: _attn_sample(lam_p, ga, gb2, q, kv, *caches))

    (x1_p, h2_p, rt_p, cnt_p), (x1_s, h2_s, rt_s, cnt_s) = mid_p, mid_s
    n_p, n_s = bp * tp, bs * ts
    tiles_p, tiles_s = n_p // TM, n_s // TM
    tiles = tiles_p + tiles_s
    cap = -(-(n_p + n_s + tiles * (SEG - 1)) // TR) * TR
    max_tiles = (TOP_K * (n_p + n_s) + tiles * N_EXPERTS * (SEG - 1)) // TR + N_EXPERTS
    cnt = jnp.concatenate([cnt_p[:, 0, :], cnt_s[:, 0, :]], axis=0).astype(jnp.int32)
    plan, sched = _plan(cnt, cap, max_tiles)
    xs, pos_p = _dispatch(plan, rt_p, h2_p.reshape(n_p, d), None, 0, cap, False)
    xs, pos_s = _dispatch(plan, rt_s, h2_s.reshape(n_s, d), xs, tiles_p, cap, True)
    ys = _experts(sched, xs, w_gate[0], w_up[0], w_down[0], max_tiles)
    y_p = _combine(plan[:3], pos_p, rt_p, x1_p, h2_p, g2[0:bp], wsg16, wsu16, wsd16, gf, ys,
                   1, TM, 0, cap)
    y_s = _combine(plan[:3], pos_s, rt_s, x1_s, h2_s, g2[bp:bp + bs], wsg16, wsu16, wsd16, gf, ys,
                   TM // ts, ts, tiles_p, cap)

    def shape_rows(rows, b, t):
        ka, va, kb, vb = rows
        return (ka.reshape(1, b, t, A_HEADS, 2 * A_QK_DIM), va.reshape(1, b, t, A_HEADS, 2 * A_QK_DIM),
                kb.reshape(1, b, t, B_HEADS, B_DIM), vb.reshape(1, b, t, B_HEADS, B_DIM))

    return (y_p, y_s) + shape_rows(rows_p, bp, tp) + shape_rows(rows_s, bs, ts)
```

```python
import functools
import math

import jax
import jax.numpy as jnp
from jax import lax
from jax.experimental import pallas as pl
from jax.experimental.pallas import tpu as pltpu

F32 = jnp.float32
BF16 = jnp.bfloat16

D_MODEL = 1024
CHUNK = 64
A_QK_DIM = 64
A_HEADS = 4
A_WIDTH = 512
B_DIM = 64
B_HEADS = 8
B_WIDTH = 512
IN_COLS = 3 * A_WIDTH + 3 * B_WIDTH
N_EXPERTS = 64
TOP_K = 6
N_GROUPS = 8
TOPK_GROUPS = 4
D_EXPERT = 256
ROUTED_SCALE = 2.5
EPS = 1e-6
NEG_INF = -1e30
LANES = 128
QK_SCALE = A_QK_DIM ** -0.5
LAM_INIT = 0.8 - 0.6 * math.exp(-0.3 * 0)
LOG2E = math.log2(math.e)
SLOPES_LOG2 = tuple(LOG2E * 2.0 ** (-8.0 * (h + 1.0) / A_HEADS) for h in range(A_HEADS))
KV_KA, KV_VA, KV_KB, KV_VB = 0, 512, 1024, 1536
VMEM_LIMIT = 56 * 1024 * 1024
TM = 256
SEG_SHIFT = 4
SEG = 1 << SEG_SHIFT
SEG_BITS = 5
CH_SHIFT = 8
CH = 1 << CH_SHIFT
RMAX = 2560
TILE_BITS = 8
TR = 1024
FILL_BITS = 6

_NT = (((1,), (1,)), ((), ()))


def _dot(a, b):
    return jnp.dot(a, b, preferred_element_type=F32)


def _dot_nt(a, b):
    return lax.dot_general(a, b, _NT, preferred_element_type=F32)


def _rms(x):
    return x * lax.rsqrt(jnp.mean(x * x, axis=-1, keepdims=True) + EPS)


def _silu(x):
    return x * jax.nn.sigmoid(x)


def _params(*sem):
    return pltpu.CompilerParams(dimension_semantics=sem, vmem_limit_bytes=VMEM_LIMIT)


def _ada_kernel(c_ref, w_ref, b_ref, o_ref):
    s = _silu(c_ref[...]).astype(BF16)
    o_ref[...] = _dot(s, w_ref[...].astype(BF16)) + b_ref[...]


def _ada(c, w, b):
    n, d = c.shape
    cols = w.shape[1]
    tn = 1536
    return pl.pallas_call(
        _ada_kernel,
        out_shape=jax.ShapeDtypeStruct((n, cols), F32),
        grid=(cols // tn,),
        in_specs=[pl.BlockSpec((n, d), lambda j: (0, 0)),
                  pl.BlockSpec((d, tn), lambda j: (0, j)),
                  pl.BlockSpec((1, tn), lambda j: (0, j))],
        out_specs=pl.BlockSpec((n, tn), lambda j: (0, j)),
        compiler_params=_params("arbitrary"),
        name="ada",
    )(c, w, b.reshape(1, cols))


def _inproj_kernel(x_ref, sc_ref, sh_ref, w_ref, q_ref, ka_ref, va_ref, kb_ref, vb_ref, kv_ref):
    bb, tt, d = x_ref.shape
    h = _rms(x_ref[...]) * (1.0 + sc_ref[...]) + sh_ref[...]
    proj = _dot(h.reshape(bb * tt, d).astype(BF16), w_ref[...])

    def part(lo, width):
        return proj[:, lo:lo + width].reshape(bb, tt, width)

    qa, ka, va = part(0, 512), part(512, 512), part(1024, 512)
    qb, kb, vb = part(1536, 512), part(2048, 512), part(2560, 512)
    q_ref[:, :, 0:512] = (qa * (QK_SCALE * LOG2E)).astype(BF16)
    q_ref[:, :, 512:1024] = (qb * (QK_SCALE * LOG2E)).astype(BF16)
    for hd in range(A_HEADS):
        ka_ref[:, pl.ds(hd, tt, stride=A_HEADS), :] = ka[:, :, hd * LANES:(hd + 1) * LANES]
        va_ref[:, pl.ds(hd, tt, stride=A_HEADS), :] = va[:, :, hd * LANES:(hd + 1) * LANES]
    kb_ref[...] = kb
    vb_ref[...] = vb
    kv_ref[:, :, KV_KA:KV_KA + 512] = ka.astype(BF16)
    kv_ref[:, :, KV_VA:KV_VA + 512] = va.astype(BF16)
    kv_ref[:, :, KV_KB:KV_KB + 512] = kb.astype(BF16)
    kv_ref[:, :, KV_VB:KV_VB + 512] = vb.astype(BF16)


def _inproj(x, sc, sh, w_bf16, bb, tt):
    b, t, d = x.shape
    xspec = pl.BlockSpec((bb, tt, d), lambda i, j: (i, j, 0))
    mspec = pl.BlockSpec((bb, 1, d), lambda i, j: (i, 0, 0))

    def ospec(width):
        return pl.BlockSpec((bb, tt, width), lambda i, j: (i, j, 0))

    f32rows = jax.ShapeDtypeStruct((b, t, 512), F32)
    a_rows = jax.ShapeDtypeStruct((b, t * A_HEADS, LANES), F32)
    a_spec = pl.BlockSpec((bb, tt * A_HEADS, LANES), lambda i, j: (i, j, 0))
    return pl.pallas_call(
        _inproj_kernel,
        out_shape=(jax.ShapeDtypeStruct((b, t, 1024), BF16), a_rows, a_rows, f32rows, f32rows,
                   jax.ShapeDtypeStruct((b, t, 2048), BF16)),
        grid=(b // bb, t // tt),
        in_specs=[xspec, mspec, mspec, pl.BlockSpec((d, IN_COLS), lambda i, j: (0, 0))],
        out_specs=(ospec(1024), a_spec, a_spec, ospec(512), ospec(512), ospec(2048)),
        compiler_params=_params("arbitrary", "arbitrary"),
        name="inproj",
    )(x, sc, sh, w_bf16)


def _lane_lo(shape):
    return lax.broadcasted_iota(jnp.int32, shape, len(shape) - 1) < 64


def _split_halves(x):
    lo = _lane_lo(x.shape)
    zero = jnp.zeros_like(x)
    return jnp.concatenate([jnp.where(lo, x, zero), jnp.where(lo, zero, x)], axis=0)


def _lam_of(lam_ref):
    lp = lam_ref[...]
    s1 = jnp.sum(lp[0:1] * lp[1:2], axis=1, keepdims=True)
    s2 = jnp.sum(lp[2:3] * lp[3:4], axis=1, keepdims=True)
    return jnp.exp(s1) - jnp.exp(s2) + LAM_INIT


def _a_update(state, s, v16):
    m, l, acc = state
    m_new = jnp.maximum(m, jnp.max(s, axis=1, keepdims=True))
    alpha = jnp.exp2(m - m_new)
    p = jnp.exp2(s - m_new)
    l = alpha * l + jnp.sum(p, axis=1, keepdims=True)
    acc = alpha * acc + _dot(p.astype(BF16), v16)
    return m_new, l, acc


def _a_init(rows):
    return (jnp.full((rows, 1), NEG_INF, F32), jnp.zeros((rows, 1), F32), jnp.zeros((rows, LANES), F32))


def _a_finish(state, lam, gain, nq):
    _, l, acc = state
    o = acc[:nq] / l[:nq] - lam * (acc[nq:] / l[nq:])
    return _rms(o) * gain * (1.0 - LAM_INIT)


def _diag_tiles(nq, nk):
    r = lax.broadcasted_iota(jnp.int32, (2 * nq, nk), 0)
    r = jnp.where(r >= nq, r - nq, r)
    c = lax.broadcasted_iota(jnp.int32, (2 * nq, nk), 1)
    visible = (c // CHUNK) <= (r // CHUNK)
    before = c < r
    dist_shift = (r - jnp.abs(r - c)).astype(F32)
    return visible, before, dist_shift


def _strict_upper(n):
    j = lax.broadcasted_iota(jnp.int32, (n, n), 0)
    s = lax.broadcasted_iota(jnp.int32, (n, n), 1)
    return jnp.where(j > s, 1.0, 0.0).astype(BF16)


def _b_block(z, v16, tri, carry, acc, before, nq, v_transposed=False):
    sp = jnp.where(z > 60.0, z, jnp.log2(1.0 + jnp.exp2(z)))
    if before is not None:
        sp = jnp.where(before, sp, 0.0)
    tk = tri.shape[0]
    sp16 = sp.astype(BF16)
    pieces, right = [], None
    for c in reversed(range(z.shape[1] // tk)):
        piece = _dot(sp16[:, c * tk:(c + 1) * tk], tri)
        pieces.append(piece if right is None else piece + right)
        chunk_sum = jnp.sum(sp[:, c * tk:(c + 1) * tk], axis=1, keepdims=True)
        right = chunk_sum if right is None else right + chunk_sum
    later = pieces[0] if len(pieces) == 1 else jnp.concatenate(pieces[::-1], axis=1)
    a = jnp.exp2((z - sp) - (carry + later))
    if before is not None:
        a = jnp.where(before, a, 0.0)
    a16 = a.astype(BF16)
    vzero = jnp.zeros_like(v16)
    if v_transposed:
        first = lax.broadcasted_iota(jnp.int32, v16.shape, 0) < B_DIM
        acc = acc + _dot_nt(a16[:nq], jnp.where(first, v16, vzero)) + _dot_nt(a16[nq:], jnp.where(first, vzero, v16))
    else:
        vlo = _lane_lo(v16.shape)
        acc = acc + _dot(a16[:nq], jnp.where(vlo, v16, vzero)) + _dot(a16[nq:], jnp.where(vlo, vzero, v16))
    return carry + right, acc


def _b_finish(acc, gain2):
    lo = _lane_lo(acc.shape)
    ss = acc * acc
    s_lo = jnp.sum(jnp.where(lo, ss, 0.0), axis=1, keepdims=True)
    s_hi = jnp.sum(jnp.where(lo, 0.0, ss), axis=1, keepdims=True)
    ms = jnp.where(lo, s_lo, s_hi) * (1.0 / B_DIM)
    return acc * lax.rsqrt(ms + EPS) * gain2


def _attn_prompt_kernel(lam_ref, ga_ref, gb_ref, q_ref, kv_ref, o_ref):
    bq = q_ref.shape[1]
    pairs = B_HEADS // 2
    i = pl.program_id(1)
    q0 = pl.multiple_of(i * bq, bq)
    lam = _lam_of(lam_ref)
    visible, before, dist_shift = _diag_tiles(bq, bq)
    col = lax.broadcasted_iota(jnp.int32, (1, bq), 1)
    tri = _strict_upper(min(bq, 256))
    qa = [_split_halves(q_ref[0, :, h * LANES:(h + 1) * LANES]) for h in range(A_HEADS)]
    qb = [_split_halves(q_ref[0, :, A_WIDTH + p * LANES:A_WIDTH + (p + 1) * LANES]) for p in range(pairs)]

    def kv(rows, base, n):
        return kv_ref[0, rows, base + n * LANES:base + (n + 1) * LANES]

    rows = pl.ds(q0, bq)
    a_st = tuple(
        _a_update(_a_init(2 * bq),
                  jnp.where(visible, _dot_nt(qa[h], kv(rows, KV_KA, h)) + SLOPES_LOG2[h] * dist_shift, NEG_INF),
                  kv(rows, KV_VA, h))
        for h in range(A_HEADS))
    b_st = tuple(
        _b_block(_dot_nt(qb[p], kv(rows, KV_KB, p)), kv(rows, KV_VB, p), tri,
                 jnp.zeros((2 * bq, 1), F32), jnp.zeros((bq, LANES), F32), before, bq)
        for p in range(pairs))

    def body(jj, st):
        a_st, b_st = st
        k0 = pl.multiple_of((i - 1 - jj) * bq, bq)
        rows = pl.ds(k0, bq)
        rel = (col + (k0 - q0)).astype(F32)
        a_new = tuple(
            _a_update(a_st[h], _dot_nt(qa[h], kv(rows, KV_KA, h)) + SLOPES_LOG2[h] * rel, kv(rows, KV_VA, h))
            for h in range(A_HEADS))
        b_new = tuple(
            _b_block(_dot_nt(qb[p], kv(rows, KV_KB, p)), kv(rows, KV_VB, p), tri, b_st[p][0], b_st[p][1], None, bq)
            for p in range(pairs))
        return a_new, b_new

    a_st, b_st = lax.fori_loop(0, i, body, (a_st, b_st))
    for h in range(A_HEADS):
        o_ref[0, :, h * LANES:(h + 1) * LANES] = _a_finish(a_st[h], lam, ga_ref[...], bq).astype(o_ref.dtype)
    for p in range(pairs):
        o_ref[0, :, A_WIDTH + p * LANES:A_WIDTH + (p + 1) * LANES] = (
            _b_finish(b_st[p][1], gb_ref[...]).astype(o_ref.dtype))


def _attn_prompt(lam_p, ga, gb2, q, kv, bq):
    b, t, _ = q.shape
    small = lambda shape: pl.BlockSpec(shape, lambda bi, i: (0, 0))
    return pl.pallas_call(
        _attn_prompt_kernel,
        out_shape=jax.ShapeDtypeStruct((b, t, D_MODEL), BF16),
        grid=(b, t // bq),
        in_specs=[small((4, A_QK_DIM)), small((1, LANES)), small((1, LANES)),
                  pl.BlockSpec((1, bq, D_MODEL), lambda bi, i: (bi, i, 0)),
                  pl.BlockSpec((1, t, 2048), lambda bi, i: (bi, 0, 0), pipeline_mode=pl.Buffered(1))],
        out_specs=pl.BlockSpec((1, bq, D_MODEL), lambda bi, i: (bi, i, 0)),
        compiler_params=_params("arbitrary", "arbitrary"),
        name="attn_prompt",
    )(lam_p, ga, gb2, q, kv)


def _attn_sample_kernel(lam_ref, ga_ref, gb_ref, q_ref, kv_ref, cak_ref, cav_ref, cbk_ref, cbv_ref, o_ref):
    nq = q_ref.shape[1]
    past = cak_ref.shape[1] // A_HEADS
    pb = 256
    lam = _lam_of(lam_ref)
    visible, before, dist_shift = _diag_tiles(nq, nq)
    pcol = lax.broadcasted_iota(jnp.int32, (1, past), 1)
    tri_new = _strict_upper(nq)
    tri_past = _strict_upper(pb)

    for h in range(A_HEADS):
        cs = slice(h * LANES, (h + 1) * LANES)
        qs = _split_halves(q_ref[0, :, cs])
        slope = SLOPES_LOG2[h]
        head_rows = pl.ds(h, past, stride=A_HEADS)
        s_past = _dot_nt(qs, cak_ref[0, head_rows, :].astype(BF16)) + slope * (pcol - past).astype(F32)
        state = _a_update(_a_init(2 * nq), s_past, cav_ref[0, head_rows, :].astype(BF16))
        k16 = kv_ref[0, :, KV_KA + h * LANES:KV_KA + (h + 1) * LANES]
        v16 = kv_ref[0, :, KV_VA + h * LANES:KV_VA + (h + 1) * LANES]
        s_new = jnp.where(visible, _dot_nt(qs, k16) + slope * dist_shift, NEG_INF)
        state = _a_update(state, s_new, v16)
        o_ref[0, :, cs] = _a_finish(state, lam, ga_ref[...], nq).astype(o_ref.dtype)

    for p in range(B_HEADS // 2):
        cs = slice(p * LANES, (p + 1) * LANES)
        qs = _split_halves(q_ref[0, :, A_WIDTH + p * LANES:A_WIDTH + (p + 1) * LANES])
        k16 = kv_ref[0, :, KV_KB + p * LANES:KV_KB + (p + 1) * LANES]
        v16 = kv_ref[0, :, KV_VB + p * LANES:KV_VB + (p + 1) * LANES]
        carry, acc = _b_block(_dot_nt(qs, k16), v16, tri_new,
                              jnp.zeros((2 * nq, 1), F32), jnp.zeros((nq, LANES), F32), before, nq)
        for j in reversed(range(past // pb)):
            keys = slice(j * pb, (j + 1) * pb)
            k_t = cbk_ref[0, 2 * p:2 * p + 2, :, keys].reshape(LANES, pb).astype(BF16)
            v_t = cbv_ref[0, 2 * p:2 * p + 2, :, keys].reshape(LANES, pb).astype(BF16)
            carry, acc = _b_block(_dot(qs, k_t), v_t, tri_past, carry, acc, None, nq, v_transposed=True)
        o_ref[0, :, A_WIDTH + p * LANES:A_WIDTH + (p + 1) * LANES] = _b_finish(acc, gb_ref[...]).astype(o_ref.dtype)


def _attn_sample(lam_p, ga, gb2, q, kv, cak, cav, cbk, cbv):
    b, t, _ = q.shape
    past = cak.shape[1] // A_HEADS
    small = lambda shape: pl.BlockSpec(shape, lambda bi: (0, 0))
    per_b = lambda rows, width: pl.BlockSpec((1, rows, width), lambda bi: (bi, 0, 0))
    a_cache = per_b(past * A_HEADS, LANES)
    b_cache = pl.BlockSpec((1, B_HEADS, B_DIM, past), lambda bi: (bi, 0, 0, 0))
    return pl.pallas_call(
        _attn_sample_kernel,
        out_shape=jax.ShapeDtypeStruct((b, t, D_MODEL), BF16),
        grid=(b,),
        in_specs=[small((4, A_QK_DIM)), small((1, LANES)), small((1, LANES)),
                  per_b(t, D_MODEL), per_b(t, 2048), a_cache, a_cache, b_cache, b_cache],
        out_specs=per_b(t, D_MODEL),
        compiler_params=_params("arbitrary"),
        name="attn_sample",
    )(lam_p, ga, gb2, q, kv, cak, cav, cbk, cbv)


def _route(logits_t, bias_col):
    n = logits_t.shape[1]
    scores = jax.nn.sigmoid(logits_t)
    sel3 = (scores + bias_col).reshape(N_GROUPS, 8, n)
    eio = lax.broadcasted_iota(jnp.int32, (N_GROUPS, 8, n), 1)
    m1 = jnp.max(sel3, axis=1, keepdims=True)
    i1 = jnp.min(jnp.where(sel3 == m1, eio, 8), axis=1, keepdims=True)
    m2 = jnp.max(jnp.where(eio == i1, -jnp.inf, sel3), axis=1, keepdims=True)
    gscore = (m1 + m2).reshape(N_GROUPS, n)
    gio = lax.broadcasted_iota(jnp.int32, (N_GROUPS, n), 0)
    keep = jnp.zeros((N_GROUPS, n), F32)
    for _ in range(TOPK_GROUPS):
        g = jnp.max(gscore, axis=0, keepdims=True)
        gi = jnp.min(jnp.where(gscore == g, gio, N_GROUPS), axis=0, keepdims=True)
        hit = gio == gi
        keep = jnp.where(hit, 1.0, keep)
        gscore = jnp.where(hit, -jnp.inf, gscore)
    cur = jnp.where(keep.reshape(N_GROUPS, 1, n) > 0.0, sel3, NEG_INF).reshape(N_EXPERTS, n)
    xio = lax.broadcasted_iota(jnp.int32, (N_EXPERTS, n), 0)
    chosen = jnp.zeros((N_EXPERTS, n), F32)
    ids, raw = [], []
    for _ in range(TOP_K):
        mx = jnp.max(cur, axis=0, keepdims=True)
        ei = jnp.min(jnp.where(cur == mx, xio, N_EXPERTS), axis=0, keepdims=True)
        hit = xio == ei
        chosen = jnp.where(hit, 1.0, chosen)
        cur = jnp.where(hit, -jnp.inf, cur)
        ids.append(ei.astype(F32))
        raw.append(jnp.sum(jnp.where(hit, scores, 0.0), axis=0, keepdims=True))
    denom = raw[0]
    for r in raw[1:]:
        denom = denom + r
    wts = [r / denom * ROUTED_SCALE for r in raw]
    pad = jnp.zeros((8 - TOP_K, n), F32)
    return jnp.concatenate(ids + [pad] + wts + [pad], axis=0), chosen


def _outproj_kernel(o_ref, x_ref, g1_ref, sc_ref, sh_ref, wout_ref, wrt_ref, br_ref, x1_ref, h2_ref, rt_ref, cnt_ref):
    bb, tt, d = x_ref.shape
    n = bb * tt
    proj = _dot(o_ref[...].reshape(n, d), wout_ref[...]).reshape(bb, tt, d)
    x1 = x_ref[...] + g1_ref[...] * proj
    x1_ref[...] = x1
    h2 = (_rms(x1) * (1.0 + sc_ref[...]) + sh_ref[...]).reshape(n, d)
    h2hi = h2.astype(BF16)
    h2_ref[...] = h2hi.reshape(bb, tt, d)
    h2lo = (h2 - h2hi.astype(F32)).astype(BF16)
    w = wrt_ref[...]
    whi = w.astype(BF16)
    wlo = (w - whi.astype(F32)).astype(BF16)
    logits_t = _dot_nt(whi, h2hi) + _dot_nt(whi, h2lo) + _dot_nt(wlo, h2hi)
    rt, chosen = _route(logits_t, br_ref[...])
    rt_ref[...] = rt
    ones = jnp.ones((8, TM), BF16)
    for j in range(n // TM):
        cnt_ref[j] = _dot_nt(ones, chosen[:, j * TM:(j + 1) * TM].astype(BF16))


def _outproj(o, x, g1, sc, sh, wout_bf16, wr_t, br_col, bb, tt):
    b, t, d = x.shape
    n = bb * tt
    steps_t = t // tt
    tok = pl.BlockSpec((bb, tt, d), lambda i, j: (i, j, 0))
    mspec = pl.BlockSpec((bb, 1, d), lambda i, j: (i, 0, 0))
    full = lambda shape: pl.BlockSpec(shape, lambda i, j: (0, 0))
    return pl.pallas_call(
        _outproj_kernel,
        out_shape=(jax.ShapeDtypeStruct((b, t, d), F32), jax.ShapeDtypeStruct((b, t, d), BF16),
                   jax.ShapeDtypeStruct((16, b * t), F32),
                   jax.ShapeDtypeStruct((b * t // TM, 8, N_EXPERTS), F32)),
        grid=(b // bb, steps_t),
        in_specs=[tok, tok, mspec, mspec, mspec, full((d, d)), full((N_EXPERTS, d)), full((N_EXPERTS, 1))],
        out_specs=(tok, tok,
                   pl.BlockSpec((16, n), lambda i, j: (0, i * steps_t + j)),
                   pl.BlockSpec((n // TM, 8, N_EXPERTS), lambda i, j: (i * steps_t + j, 0, 0))),
        compiler_params=_params("arbitrary", "arbitrary"),
        name="outproj_router",
    )(o, x, g1, sc, sh, wout_bf16, wr_t, br_col)


FAST_BIT = 1
EXPERT_UNROLL = 4


def _seg_chunks(seg_ref, np_ref, base_ref, anchor, g, cap, visit):
    def body(e4, c):
        def chunk(seg, b):
            s0, n16, b0 = seg
            off = (n16 & ((1 << b) - 1)) * SEG

            @pl.when(((n16 >> b) & 1) == 1)
            def _():
                visit(pl.multiple_of(s0 + off, SEG), pl.multiple_of(b0 + off, SEG), SEG << b)

        segs = []
        for u in range(EXPERT_UNROLL):
            e = e4 * EXPERT_UNROLL + u
            idx = g * N_EXPERTS + e
            segs.append((seg_ref[idx], np_ref[idx], e * cap + base_ref[idx]))
        for s0, n16, b0 in segs:
            @pl.when(n16 == (1 << FAST_BIT))
            def _(s0=s0, b0=b0):
                visit(pl.multiple_of(s0, SEG), pl.multiple_of(b0, SEG), SEG << FAST_BIT)
        for seg in segs:
            @pl.when(jnp.logical_and(seg[1] != (1 << FAST_BIT), seg[1] != 0))
            def _(seg=seg):
                anchor[...] = jnp.zeros_like(anchor)
                for b in range(SEG_BITS):
                    chunk(seg, b)
        return c
    lax.fori_loop(0, N_EXPERTS // EXPERT_UNROLL, body, 0)


def _tile_rows(seg_ref, np_ref, g):
    last = g * N_EXPERTS + N_EXPERTS - 1
    return seg_ref[last] + np_ref[last] * SEG


def _wait_tile(seg_ref, np_ref, g, wait_rows):
    n16 = _tile_rows(seg_ref, np_ref, g) >> SEG_SHIFT
    for b in range(TILE_BITS):
        @pl.when(((n16 >> b) & 1) == 1)
        def _(b=b):
            wait_rows(SEG << b)


def _dispatch_kernel(seg_ref, np_ref, base_ref, fin_ref, rem_ref, rt_ref, h2_ref, *rest, t0, nsteps, cap, fill):
    xs_ref, pos_ref, buf, zbuf, anchor, sem, zsem = rest[-7:]
    s = pl.program_id(0)
    g = t0 + s
    slot = s % 2

    def copy(slot_, row, hrow, size):
        return pltpu.make_async_copy(buf.at[slot_, pl.ds(row, size)], xs_ref.at[pl.ds(hrow, size)], sem.at[slot_])

    def chunks(g_, fn):
        _seg_chunks(seg_ref, np_ref, base_ref, anchor, g_, cap, fn)

    def wait_tile(g_, slot_):
        _wait_tile(seg_ref, np_ref, g_, lambda n: copy(slot_, 0, 0, n).wait())

    @pl.when(s >= 2)
    def _():
        wait_tile(g - 2, slot)

    rt = rt_ref[...]
    eio = lax.broadcasted_iota(jnp.int32, (N_EXPERTS, TM), 0).astype(F32)
    chosen = jnp.zeros((N_EXPERTS, TM), F32)
    for k in range(TOP_K):
        chosen = jnp.where(eio == rt[k:k + 1], 1.0, chosen)
    t_row = lax.broadcasted_iota(jnp.int32, (TM, TM), 0)
    t_col = lax.broadcasted_iota(jnp.int32, (TM, TM), 1)
    rank = _dot(chosen.astype(BF16), jnp.where(t_row < t_col, 1.0, 0.0).astype(BF16))
    n_e = jnp.sum(chosen, axis=1, keepdims=True).astype(jnp.int32)
    np16 = ((n_e + (SEG - 1)) >> SEG_SHIFT).astype(F32)
    e_row = lax.broadcasted_iota(jnp.int32, (N_EXPERTS, N_EXPERTS), 0)
    e_col = lax.broadcasted_iota(jnp.int32, (N_EXPERTS, N_EXPERTS), 1)
    lower = jnp.where(e_col < e_row, 1.0, 0.0).astype(BF16)
    seg0 = SEG * _dot(lower, jnp.broadcast_to(np16, (N_EXPERTS, LANES)).astype(BF16))[:, 0:1]
    pos = seg0 + rank
    pos_rows = [jnp.sum(jnp.where(eio == rt[k:k + 1], pos, 0.0), axis=0, keepdims=True) for k in range(TOP_K)]
    pos_ref[...] = jnp.concatenate(pos_rows + [jnp.zeros((8 - TOP_K, TM), F32)], axis=0)

    h2 = h2_ref[...]
    npairs = (_tile_rows(seg_ref, np_ref, g) + (2 * CH - 1)) >> (CH_SHIFT + 1)

    def chunk_pair(c2, carry):
        for half in range(2):
            r0 = pl.multiple_of(c2 * (2 * CH) + half * CH, CH)
            rio = (lax.broadcasted_iota(jnp.int32, (CH, TM), 0) + r0).astype(F32)
            p = jnp.zeros((CH, TM), F32)
            for k in range(TOP_K):
                p = jnp.where(rio == pos_rows[k], 1.0, p)
            buf[slot, pl.ds(r0, CH), :] = _dot(p.astype(BF16), h2).astype(BF16)
        return carry
    lax.fori_loop(0, npairs, chunk_pair, 0)

    chunks(g, lambda r, hr, n: copy(slot, r, hr, n).start())

    @pl.when(s == nsteps - 1)
    def _():
        if nsteps >= 2:
            wait_tile(g - 1, 1 - slot)
        wait_tile(g, slot)
        if fill:
            zbuf[...] = jnp.zeros_like(zbuf)

            def zero_chunks(fn):
                def body(e, c):
                    r16 = rem_ref[e]
                    h0 = e * cap + fin_ref[e]
                    for b in range(FILL_BITS):
                        off = (r16 & ((1 << b) - 1)) * SEG

                        @pl.when(((r16 >> b) & 1) == 1)
                        def _(b=b, off=off):
                            size = SEG << b
                            fn(pltpu.make_async_copy(zbuf.at[pl.ds(0, size)],
                                                     xs_ref.at[pl.ds(pl.multiple_of(h0 + off, SEG), size)], zsem))
                    return c
                lax.fori_loop(0, N_EXPERTS, body, 0)
            zero_chunks(lambda cp: cp.start())
            zero_chunks(lambda cp: cp.wait())


def _dispatch(plan, rt, h2, xs, t0, cap, fill):
    n, d = h2.shape
    nsteps = n // TM
    kern = functools.partial(_dispatch_kernel, t0=t0, nsteps=nsteps, cap=cap, fill=fill)
    in_specs = [pl.BlockSpec((16, TM), lambda s, *_: (0, s)), pl.BlockSpec((TM, d), lambda s, *_: (s, 0))]
    args = [rt, h2]
    aliases = {}
    if xs is not None:
        in_specs.append(pl.BlockSpec(memory_space=pl.ANY))
        args.append(xs)
        aliases = {len(plan) + 2: 0}
    return pl.pallas_call(
        kern,
        out_shape=(jax.ShapeDtypeStruct((N_EXPERTS * cap, d), BF16), jax.ShapeDtypeStruct((8, n), F32)),
        grid_spec=pltpu.PrefetchScalarGridSpec(
            num_scalar_prefetch=len(plan), grid=(nsteps,), in_specs=in_specs,
            out_specs=(pl.BlockSpec(memory_space=pl.ANY), pl.BlockSpec((8, TM), lambda s, *_: (0, s))),
            scratch_shapes=[pltpu.VMEM((2, RMAX, d), BF16), pltpu.VMEM((SEG << (FILL_BITS - 1), d), BF16),
                            pltpu.VMEM((8, LANES), F32),
                            pltpu.SemaphoreType.DMA((2,)), pltpu.SemaphoreType.DMA(())]),
        input_output_aliases=aliases,
        compiler_params=_params("arbitrary"),
        name="dispatch",
    )(*plan, *args)


def _expert_kernel(te_ref, tb_ref, tv_ref, tf_ref, xs_ref, wg_ref, wu_ref, wd_ref, ys_ref, wg16, wu16, wd16):
    i = pl.program_id(0)

    @pl.when(tf_ref[i] == 1)
    def _():
        wg16[...] = wg_ref[0].astype(BF16)
        wu16[...] = wu_ref[0].astype(BF16)
        wd16[...] = wd_ref[0].astype(BF16)

    @pl.when(tv_ref[i] == 1)
    def _():
        x = xs_ref[...]
        act = _silu(_dot(x, wg16[...])) * _dot(x, wu16[...])
        ys_ref[...] = _dot(act.astype(BF16), wd16[...]).astype(BF16)


def _experts(sched, xs, wg, wu, wd, max_tiles):
    d = xs.shape[1]
    wspec = lambda shape: pl.BlockSpec((1,) + shape, lambda i, te, tb, tv, tf: (te[i], 0, 0))
    row_map = lambda i, te, tb, tv, tf: (tb[i], 0)
    return pl.pallas_call(
        _expert_kernel,
        out_shape=jax.ShapeDtypeStruct(xs.shape, BF16),
        grid_spec=pltpu.PrefetchScalarGridSpec(
            num_scalar_prefetch=4, grid=(max_tiles,),
            in_specs=[pl.BlockSpec((TR, d), row_map),
                      wspec((d, D_EXPERT)), wspec((d, D_EXPERT)), wspec((D_EXPERT, d))],
            out_specs=pl.BlockSpec((TR, d), row_map),
            scratch_shapes=[pltpu.VMEM((d, D_EXPERT), BF16), pltpu.VMEM((d, D_EXPERT), BF16),
                            pltpu.VMEM((D_EXPERT, d), BF16)]),
        compiler_params=_params("arbitrary"),
        name="experts",
    )(*sched, xs, wg, wu, wd)


def _combine_kernel(seg_ref, np_ref, base_ref, pos_ref, rt_ref, x1_ref, h2_ref, g2_ref, wsg_ref, wsu_ref, wsd_ref,
                    gf_ref, ys_ref, y_ref, ybuf, acc, anchor, sem, *, t0, nsteps, cap):
    bb, tt, d = x1_ref.shape
    s = pl.program_id(0)
    g = t0 + s
    slot = s % 2

    def copy(slot_, row, hrow, size):
        return pltpu.make_async_copy(ys_ref.at[pl.ds(hrow, size)], ybuf.at[slot_, pl.ds(row, size)], sem.at[slot_])

    def chunks(g_, fn):
        _seg_chunks(seg_ref, np_ref, base_ref, anchor, g_, cap, fn)

    @pl.when(s == 0)
    def _():
        ybuf[...] = jnp.zeros_like(ybuf)
        chunks(g, lambda r, hr, n: copy(0, r, hr, n).start())

    @pl.when(s + 1 < nsteps)
    def _():
        chunks(g + 1, lambda r, hr, n: copy(1 - slot, r, hr, n).start())

    _wait_tile(seg_ref, np_ref, g, lambda n: copy(slot, 0, 0, n).wait())

    stacked = jnp.concatenate([pos_ref[...], jnp.zeros((8, TM), F32), rt_ref[...],
                               jnp.zeros((LANES - 32, TM), F32)], axis=0)
    cols = stacked.T
    pos_b = [jnp.broadcast_to(cols[:, k:k + 1], (TM, LANES)) for k in range(TOP_K)]
    wt_b = [jnp.broadcast_to(cols[:, 24 + k:25 + k], (TM, LANES)) for k in range(TOP_K)]
    npairs = (_tile_rows(seg_ref, np_ref, g) + (2 * CH - 1)) >> (CH_SHIFT + 1)
    acc[...] = jnp.zeros_like(acc)

    def chunk_pair(c2, carry):
        part = None
        for half in range(2):
            r0 = pl.multiple_of(c2 * (2 * CH) + half * CH, CH)
            w_parts = []
            for lane0 in range(0, CH, LANES):
                rio = (lax.broadcasted_iota(jnp.int32, (TM, LANES), 1) + (r0 + lane0)).astype(F32)
                w = jnp.zeros((TM, LANES), F32)
                for k in range(TOP_K):
                    w = jnp.where(rio == pos_b[k], wt_b[k], w)
                w_parts.append(w.astype(BF16))
            prod = _dot(jnp.concatenate(w_parts, axis=1), ybuf[slot, pl.ds(r0, CH), :])
            part = prod if part is None else part + prod
        acc[...] += part
        return carry
    lax.fori_loop(0, npairs, chunk_pair, 0)

    h = h2_ref[...].reshape(bb * tt, d)
    act = _silu(_dot(h, wsg_ref[...])) * _dot(h, wsu_ref[...])
    moe = acc[...] + _dot(act.astype(BF16), wsd_ref[...])
    x2 = x1_ref[...] + g2_ref[...] * moe.reshape(bb, tt, d)
    y_ref[...] = _rms(x2) * gf_ref[...]


def _combine(plan3, pos, rt, x1, h2, g2, wsg, wsu, wsd, gf, ys, bb, tt, t0, cap):
    b, t, d = x1.shape
    steps_t = t // tt
    nsteps = (b // bb) * steps_t
    bidx = lambda s: (s // steps_t, s % steps_t, 0)
    tok = pl.BlockSpec((bb, tt, d), lambda s, *_: bidx(s))
    full = lambda shape: pl.BlockSpec(shape, lambda s, *_: (0,) * len(shape))
    kern = functools.partial(_combine_kernel, t0=t0, nsteps=nsteps, cap=cap)
    return pl.pallas_call(
        kern,
        out_shape=jax.ShapeDtypeStruct((b, t, d), F32),
        grid_spec=pltpu.PrefetchScalarGridSpec(
            num_scalar_prefetch=3, grid=(nsteps,),
            in_specs=[pl.BlockSpec((8, TM), lambda s, *_: (0, s)), pl.BlockSpec((16, TM), lambda s, *_: (0, s)),
                      tok, tok, pl.BlockSpec((bb, 1, d), lambda s, *_: (s // steps_t, 0, 0)),
                      full(wsg.shape), full(wsu.shape), full(wsd.shape), full((1, 1, d)),
                      pl.BlockSpec(memory_space=pl.ANY)],
            out_specs=tok,
            scratch_shapes=[pltpu.VMEM((2, RMAX, d), BF16), pltpu.VMEM((TM, d), F32),
                            pltpu.VMEM((8, LANES), F32), pltpu.SemaphoreType.DMA((2,))]),
        compiler_params=_params("arbitrary"),
        name="combine",
    )(*plan3, pos, rt, x1, h2, g2, wsg, wsu, wsd, gf, ys)


def _plan(cnt, cap, max_tiles):
    np16 = (cnt + (SEG - 1)) // SEG
    npad = np16 * SEG
    seg = jnp.cumsum(npad, axis=1) - npad
    base = jnp.cumsum(npad, axis=0) - npad
    fin = jnp.sum(npad, axis=0)
    rem16 = ((-fin) % TR) // SEG
    tiles_e = (fin + (TR - 1)) // TR
    ends = jnp.cumsum(tiles_e)
    i = jnp.arange(max_tiles, dtype=jnp.int32)
    valid = i < ends[-1]
    ii = jnp.where(valid, i, jnp.maximum(ends[-1] - 1, 0))
    te = jnp.sum((ends[None, :] <= ii[:, None]).astype(jnp.int32), axis=1)
    te = jnp.minimum(te, N_EXPERTS - 1)
    owner = jnp.arange(N_EXPERTS, dtype=jnp.int32)[None, :] == te[:, None]
    r = ii - jnp.sum(jnp.where(owner, (ends - tiles_e)[None, :], 0), axis=1)
    tb = te * (cap // TR) + r
    first = jnp.logical_and(valid, r == 0)
    i32 = lambda a: a.astype(jnp.int32).reshape(-1)
    return (i32(seg), i32(np16), i32(base), i32(fin), i32(rem16)), (te, i32(tb), i32(valid), i32(first))


def kernel(x_prompt, x_sample, cache_a_k, cache_a_v, cache_b_k, cache_b_v, c_prompt, c_sample, w_ada, b_ada, w_in, lam_params, g_sub_a, g_sub_b, w_out, w_router, b_router, w_gate, w_up, w_down, w_shared_gate, w_shared_up, w_shared_down, g_final):
    bp, tp, d = x_prompt.shape
    bs, ts, _ = x_sample.shape
    past = cache_a_k.shape[2]

    mod = _ada(jnp.concatenate([c_prompt, c_sample], axis=0), w_ada[0], b_ada[0])
    mod = mod.reshape(bp + bs, 6, 1, d)
    sh1, sc1, g1, sh2, sc2, g2 = (mod[:, k] for k in range(6))

    w_in16 = w_in[0].astype(BF16)
    w_out16 = w_out[0].astype(BF16)
    wr_t = w_router[0].T
    br_col = b_router[0].reshape(N_EXPERTS, 1)
    wsg16, wsu16, wsd16 = (w_shared_gate[0].astype(BF16), w_shared_up[0].astype(BF16),
                           w_shared_down[0].astype(BF16))
    ga = g_sub_a[0].reshape(1, LANES)
    gb2 = jnp.concatenate([g_sub_b[0], g_sub_b[0]]).reshape(1, LANES)
    gf = g_final.reshape(1, 1, d)
    lam_p = lam_params[0]

    def front(x, sl, bb, tt, attn):
        q, ka, va, kb, vb, kv = _inproj(x, sc1[sl], sh1[sl], w_in16, bb, tt)
        o = attn(q, kv)
        x1, h2, rt, cnt = _outproj(o, x, g1[sl], sc2[sl], sh2[sl], w_out16, wr_t, br_col, bb, tt)
        return (x1, h2, rt, cnt), (ka, va, kb, vb)

    mid_p, rows_p = front(x_prompt, slice(0, bp), 1, 512,
                          lambda q, kv: _attn_prompt(lam_p, ga, gb2, q, kv, 512))
    caches = (cache_a_k.reshape(bs, past * A_HEADS, LANES), cache_a_v.reshape(bs, past * A_HEADS, LANES),
              jnp.transpose(cache_b_k[0], (0, 2, 3, 1)), jnp.transpose(cache_b_v[0], (0, 2, 3, 1)))
    mid_s, rows_s = front(x_sample, slice(bp, bp + bs), 8, ts,
                          lambda q, kv: _attn_sample(lam_p, ga, gb2, q, kv, *caches))

    (x1_p, h2_p, rt_p, cnt_p), (x1_s, h2_s, rt_s, cnt_s) = mid_p, mid_s
    n_p, n_s = bp * tp, bs * ts
    tiles_p, tiles_s = n_p // TM, n_s // TM
    tiles = tiles_p + tiles_s
    cap = -(-(n_p + n_s + tiles * (SEG - 1)) // TR) * TR
    max_tiles = (TOP_K * (n_p + n_s) + tiles * N_EXPERTS * (SEG - 1)) // TR + N_EXPERTS
    cnt = jnp.concatenate([cnt_p[:, 0, :], cnt_s[:, 0, :]], axis=0).astype(jnp.int32)
    plan, sched = _plan(cnt, cap, max_tiles)
    xs, pos_p = _dispatch(plan, rt_p, h2_p.reshape(n_p, d), None, 0, cap, False)
    xs, pos_s = _dispatch(plan, rt_s, h2_s.reshape(n_s, d), xs, tiles_p, cap, True)
    ys = _experts(sched, xs, w_gate[0], w_up[0], w_down[0], max_tiles)
    y_p = _combine(plan[:3], pos_p, rt_p, x1_p, h2_p, g2[0:bp], wsg16, wsu16, wsd16, gf, ys,
                   1, TM, 0, cap)
    y_s = _combine(plan[:3], pos_s, rt_s, x1_s, h2_s, g2[bp:bp + bs], wsg16, wsu16, wsd16, gf, ys,
                   TM // ts, ts, tiles_p, cap)

    def shape_rows(rows, b, t):
        ka, va, kb, vb = rows
        return (ka.reshape(1, b, t, A_HEADS, 2 * A_QK_DIM), va.reshape(1, b, t, A_HEADS, 2 * A_QK_DIM),
                kb.reshape(1, b, t, B_HEADS, B_DIM), vb.reshape(1, b, t, B_HEADS, B_DIM))

    return (y_p, y_s) + shape_rows(rows_p, bp, tp) + shape_rows(rows_s, bs, ts)
```

```python
import functools
import math

import jax
import jax.numpy as jnp
from jax import lax
from jax.experimental import pallas as pl
from jax.experimental.pallas import tpu as pltpu

F32 = jnp.float32
BF16 = jnp.bfloat16

D_MODEL = 1024
CHUNK = 64
A_QK_DIM = 64
A_HEADS = 4
A_WIDTH = 512
B_DIM = 64
B_HEADS = 8
B_WIDTH = 512
IN_COLS = 3 * A_WIDTH + 3 * B_WIDTH
N_EXPERTS = 64
TOP_K = 6
N_GROUPS = 8
TOPK_GROUPS = 4
D_EXPERT = 256
ROUTED_SCALE = 2.5
EPS = 1e-6
NEG_INF = -1e30
LANES = 128
QK_SCALE = A_QK_DIM ** -0.5
LAM_INIT = 0.8 - 0.6 * math.exp(-0.3 * 0)
LOG2E = math.log2(math.e)
SLOPES_LOG2 = tuple(LOG2E * 2.0 ** (-8.0 * (h + 1.0) / A_HEADS) for h in range(A_HEADS))
KV_KA, KV_VA, KV_KB, KV_VB = 0, 512, 1024, 1536
VMEM_LIMIT = 56 * 1024 * 1024
TM = 256
SEG_SHIFT = 4
SEG = 1 << SEG_SHIFT
SEG_BITS = 5
CH_SHIFT = 8
CH = 1 << CH_SHIFT
RMAX = 2560
TILE_BITS = 8
TR = 1024
FILL_BITS = 6

_NT = (((1,), (1,)), ((), ()))


def _dot(a, b):
    return jnp.dot(a, b, preferred_element_type=F32)


def _dot_nt(a, b):
    return lax.dot_general(a, b, _NT, preferred_element_type=F32)


def _rms(x):
    return x * lax.rsqrt(jnp.mean(x * x, axis=-1, keepdims=True) + EPS)


def _silu(x):
    return x * jax.nn.sigmoid(x)


def _params(*sem):
    return pltpu.CompilerParams(dimension_semantics=sem, vmem_limit_bytes=VMEM_LIMIT)


def _ada_kernel(c_ref, w_ref, b_ref, o_ref):
    s = _silu(c_ref[...]).astype(BF16)
    o_ref[...] = _dot(s, w_ref[...].astype(BF16)) + b_ref[...]


def _ada(c, w, b):
    n, d = c.shape
    cols = w.shape[1]
    tn = 1536
    return pl.pallas_call(
        _ada_kernel,
        out_shape=jax.ShapeDtypeStruct((n, cols), F32),
        grid=(cols // tn,),
        in_specs=[pl.BlockSpec((n, d), lambda j: (0, 0)),
                  pl.BlockSpec((d, tn), lambda j: (0, j)),
                  pl.BlockSpec((1, tn), lambda j: (0, j))],
        out_specs=pl.BlockSpec((n, tn), lambda j: (0, j)),
        compiler_params=_params("arbitrary"),
        name="ada",
    )(c, w, b.reshape(1, cols))


def _inproj_kernel(x_ref, sc_ref, sh_ref, w_ref, q_ref, ka_ref, va_ref, kb_ref, vb_ref, kv_ref):
    bb, tt, d = x_ref.shape
    h = _rms(x_ref[...]) * (1.0 + sc_ref[...]) + sh_ref[...]
    proj = _dot(h.reshape(bb * tt, d).astype(BF16), w_ref[...])

    def part(lo, width):
        return proj[:, lo:lo + width].reshape(bb, tt, width)

    qa, ka, va = part(0, 512), part(512, 512), part(1024, 512)
    qb, kb, vb = part(1536, 512), part(2048, 512), part(2560, 512)
    q_ref[:, :, 0:512] = (qa * (QK_SCALE * LOG2E)).astype(BF16)
    q_ref[:, :, 512:1024] = (qb * (QK_SCALE * LOG2E)).astype(BF16)
    for hd in range(A_HEADS):
        ka_ref[:, pl.ds(hd, tt, stride=A_HEADS), :] = ka[:, :, hd * LANES:(hd + 1) * LANES]
        va_ref[:, pl.ds(hd, tt, stride=A_HEADS), :] = va[:, :, hd * LANES:(hd + 1) * LANES]
    kb_ref[...] = kb
    vb_ref[...] = vb
    kv_ref[:, :, KV_KA:KV_KA + 512] = ka.astype(BF16)
    kv_ref[:, :, KV_VA:KV_VA + 512] = va.astype(BF16)
    kv_ref[:, :, KV_KB:KV_KB + 512] = kb.astype(BF16)
    kv_ref[:, :, KV_VB:KV_VB + 512] = vb.astype(BF16)


def _inproj(x, sc, sh, w_bf16, bb, tt):
    b, t, d = x.shape
    xspec = pl.BlockSpec((bb, tt, d), lambda i, j: (i, j, 0))
    mspec = pl.BlockSpec((bb, 1, d), lambda i, j: (i, 0, 0))

    def ospec(width):
        return pl.BlockSpec((bb, tt, width), lambda i, j: (i, j, 0))

    f32rows = jax.ShapeDtypeStruct((b, t, 512), F32)
    a_rows = jax.ShapeDtypeStruct((b, t * A_HEADS, LANES), F32)
    a_spec = pl.BlockSpec((bb, tt * A_HEADS, LANES), lambda i, j: (i, j, 0))
    return pl.pallas_call(
        _inproj_kernel,
        out_shape=(jax.ShapeDtypeStruct((b, t, 1024), BF16), a_rows, a_rows, f32rows, f32rows,
                   jax.ShapeDtypeStruct((b, t, 2048), BF16)),
        grid=(b // bb, t // tt),
        in_specs=[xspec, mspec, mspec, pl.BlockSpec((d, IN_COLS), lambda i, j: (0, 0))],
        out_specs=(ospec(1024), a_spec, a_spec, ospec(512), ospec(512), ospec(2048)),
        compiler_params=_params("arbitrary", "arbitrary"),
        name="inproj",
    )(x, sc, sh, w_bf16)


def _lane_lo(shape):
    return lax.broadcasted_iota(jnp.int32, shape, len(shape) - 1) < 64


def _split_halves(x):
    lo = _lane_lo(x.shape)
    zero = jnp.zeros_like(x)
    return jnp.concatenate([jnp.where(lo, x, zero), jnp.where(lo, zero, x)], axis=0)


def _lam_of(lam_ref):
    lp = lam_ref[...]
    s1 = jnp.sum(lp[0:1] * lp[1:2], axis=1, keepdims=True)
    s2 = jnp.sum(lp[2:3] * lp[3:4], axis=1, keepdims=True)
    return jnp.exp(s1) - jnp.exp(s2) + LAM_INIT


def _a_update(state, s, v16):
    m, l, acc = state
    m_new = jnp.maximum(m, jnp.max(s, axis=1, keepdims=True))
    alpha = jnp.exp2(m - m_new)
    p = jnp.exp2(s - m_new)
    l = alpha * l + jnp.sum(p, axis=1, keepdims=True)
    acc = alpha * acc + _dot(p.astype(BF16), v16)
    return m_new, l, acc


def _a_init(rows):
    return (jnp.full((rows, 1), NEG_INF, F32), jnp.zeros((rows, 1), F32), jnp.zeros((rows, LANES), F32))


def _a_finish(state, lam, gain, nq):
    _, l, acc = state
    o = acc[:nq] / l[:nq] - lam * (acc[nq:] / l[nq:])
    return _rms(o) * gain * (1.0 - LAM_INIT)


def _diag_tiles(nq, nk):
    r = lax.broadcasted_iota(jnp.int32, (2 * nq, nk), 0)
    r = jnp.where(r >= nq, r - nq, r)
    c = lax.broadcasted_iota(jnp.int32, (2 * nq, nk), 1)
    visible = (c // CHUNK) <= (r // CHUNK)
    before = c < r
    dist_shift = (r - jnp.abs(r - c)).astype(F32)
    return visible, before, dist_shift


def _strict_upper(n):
    j = lax.broadcasted_iota(jnp.int32, (n, n), 0)
    s = lax.broadcasted_iota(jnp.int32, (n, n), 1)
    return jnp.where(j > s, 1.0, 0.0).astype(BF16)


def _b_block(z, v16, tri, carry, acc, before, nq, v_transposed=False):
    sp = jnp.where(z > 60.0, z, jnp.log2(1.0 + jnp.exp2(z)))
    if before is not None:
        sp = jnp.where(before, sp, 0.0)
    tk = tri.shape[0]
    sp16 = sp.astype(BF16)
    pieces, right = [], None
    for c in reversed(range(z.shape[1] // tk)):
        piece = _dot(sp16[:, c * tk:(c + 1) * tk], tri)
        pieces.append(piece if right is None else piece + right)
        chunk_sum = jnp.sum(sp[:, c * tk:(c + 1) * tk], axis=1, keepdims=True)
        right = chunk_sum if right is None else right + chunk_sum
    later = pieces[0] if len(pieces) == 1 else jnp.concatenate(pieces[::-1], axis=1)
    a = jnp.exp2((z - sp) - (carry + later))
    if before is not None:
        a = jnp.where(before, a, 0.0)
    a16 = a.astype(BF16)
    vzero = jnp.zeros_like(v16)
    if v_transposed:
        first = lax.broadcasted_iota(jnp.int32, v16.shape, 0) < B_DIM
        acc = acc + _dot_nt(a16[:nq], jnp.where(first, v16, vzero)) + _dot_nt(a16[nq:], jnp.where(first, vzero, v16))
    else:
        vlo = _lane_lo(v16.shape)
        acc = acc + _dot(a16[:nq], jnp.where(vlo, v16, vzero)) + _dot(a16[nq:], jnp.where(vlo, vzero, v16))
    return carry + right, acc


def _b_finish(acc, gain2):
    lo = _lane_lo(acc.shape)
    ss = acc * acc
    s_lo = jnp.sum(jnp.where(lo, ss, 0.0), axis=1, keepdims=True)
    s_hi = jnp.sum(jnp.where(lo, 0.0, ss), axis=1, keepdims=True)
    ms = jnp.where(lo, s_lo, s_hi) * (1.0 / B_DIM)
    return acc * lax.rsqrt(ms + EPS) * gain2


def _attn_prompt_kernel(lam_ref, ga_ref, gb_ref, q_ref, kv_ref, o_ref):
    bq = q_ref.shape[1]
    pairs = B_HEADS // 2
    i = pl.program_id(1)
    q0 = pl.multiple_of(i * bq, bq)
    lam = _lam_of(lam_ref)
    visible, before, dist_shift = _diag_tiles(bq, bq)
    col = lax.broadcasted_iota(jnp.int32, (1, bq), 1)
    tri = _strict_upper(min(bq, 256))
    qa = [_split_halves(q_ref[0, :, h * LANES:(h + 1) * LANES]) for h in range(A_HEADS)]
    qb = [_split_halves(q_ref[0, :, A_WIDTH + p * LANES:A_WIDTH + (p + 1) * LANES]) for p in range(pairs)]

    def kv(rows, base, n):
        return kv_ref[0, rows, base + n * LANES:base + (n + 1) * LANES]

    rows = pl.ds(q0, bq)
    a_st = tuple(
        _a_update(_a_init(2 * bq),
                  jnp.where(visible, _dot_nt(qa[h], kv(rows, KV_KA, h)) + SLOPES_LOG2[h] * dist_shift, NEG_INF),
                  kv(rows, KV_VA, h))
        for h in range(A_HEADS))
    b_st = tuple(
        _b_block(_dot_nt(qb[p], kv(rows, KV_KB, p)), kv(rows, KV_VB, p), tri,
                 jnp.zeros((2 * bq, 1), F32), jnp.zeros((bq, LANES), F32), before, bq)
        for p in range(pairs))

    def body(jj, st):
        a_st, b_st = st
        k0 = pl.multiple_of((i - 1 - jj) * bq, bq)
        rows = pl.ds(k0, bq)
        rel = (col + (k0 - q0)).astype(F32)
        a_new = tuple(
            _a_update(a_st[h], _dot_nt(qa[h], kv(rows, KV_KA, h)) + SLOPES_LOG2[h] * rel, kv(rows, KV_VA, h))
            for h in range(A_HEADS))
        b_new = tuple(
            _b_block(_dot_nt(qb[p], kv(rows, KV_KB, p)), kv(rows, KV_VB, p), tri, b_st[p][0], b_st[p][1], None, bq)
            for p in range(pairs))
        return a_new, b_new

    a_st, b_st = lax.fori_loop(0, i, body, (a_st, b_st))
    for h in range(A_HEADS):
        o_ref[0, :, h * LANES:(h + 1) * LANES] = _a_finish(a_st[h], lam, ga_ref[...], bq).astype(o_ref.dtype)
    for p in range(pairs):
        o_ref[0, :, A_WIDTH + p * LANES:A_WIDTH + (p + 1) * LANES] = (
            _b_finish(b_st[p][1], gb_ref[...]).astype(o_ref.dtype))


def _attn_prompt(lam_p, ga, gb2, q, kv, bq):
    b, t, _ = q.shape
    small = lambda shape: pl.BlockSpec(shape, lambda bi, i: (0, 0))
    return pl.pallas_call(
        _attn_prompt_kernel,
        out_shape=jax.ShapeDtypeStruct((b, t, D_MODEL), BF16),
        grid=(b, t // bq),
        in_specs=[small((4, A_QK_DIM)), small((1, LANES)), small((1, LANES)),
                  pl.BlockSpec((1, bq, D_MODEL), lambda bi, i: (bi, i, 0)),
                  pl.BlockSpec((1, t, 2048), lambda bi, i: (bi, 0, 0))],
        out_specs=pl.BlockSpec((1, bq, D_MODEL), lambda bi, i: (bi, i, 0)),
        compiler_params=pltpu.CompilerParams(dimension_semantics=("arbitrary", "arbitrary"),
                                             vmem_limit_bytes=62 * 1024 * 1024),
        name="attn_prompt",
    )(lam_p, ga, gb2, q, kv)


def _attn_sample_kernel(lam_ref, ga_ref, gb_ref, q_ref, kv_ref, cak_ref, cav_ref, cbk_ref, cbv_ref, o_ref):
    nq = q_ref.shape[1]
    past = cak_ref.shape[1] // A_HEADS
    pb = 256
    lam = _lam_of(lam_ref)
    visible, before, dist_shift = _diag_tiles(nq, nq)
    pcol = lax.broadcasted_iota(jnp.int32, (1, past), 1)
    tri_new = _strict_upper(nq)
    tri_past = _strict_upper(pb)

    for h in range(A_HEADS):
        cs = slice(h * LANES, (h + 1) * LANES)
        qs = _split_halves(q_ref[0, :, cs])
        slope = SLOPES_LOG2[h]
        head_rows = pl.ds(h, past, stride=A_HEADS)
        s_past = _dot_nt(qs, cak_ref[0, head_rows, :].astype(BF16)) + slope * (pcol - past).astype(F32)
        state = _a_update(_a_init(2 * nq), s_past, cav_ref[0, head_rows, :].astype(BF16))
        k16 = kv_ref[0, :, KV_KA + h * LANES:KV_KA + (h + 1) * LANES]
        v16 = kv_ref[0, :, KV_VA + h * LANES:KV_VA + (h + 1) * LANES]
        s_new = jnp.where(visible, _dot_nt(qs, k16) + slope * dist_shift, NEG_INF)
        state = _a_update(state, s_new, v16)
        o_ref[0, :, cs] = _a_finish(state, lam, ga_ref[...], nq).astype(o_ref.dtype)

    for p in range(B_HEADS // 2):
        cs = slice(p * LANES, (p + 1) * LANES)
        qs = _split_halves(q_ref[0, :, A_WIDTH + p * LANES:A_WIDTH + (p + 1) * LANES])
        k16 = kv_ref[0, :, KV_KB + p * LANES:KV_KB + (p + 1) * LANES]
        v16 = kv_ref[0, :, KV_VB + p * LANES:KV_VB + (p + 1) * LANES]
        carry, acc = _b_block(_dot_nt(qs, k16), v16, tri_new,
                              jnp.zeros((2 * nq, 1), F32), jnp.zeros((nq, LANES), F32), before, nq)
        for j in reversed(range(past // pb)):
            keys = slice(j * pb, (j + 1) * pb)
            k_t = cbk_ref[0, 2 * p:2 * p + 2, :, keys].reshape(LANES, pb).astype(BF16)
            v_t = cbv_ref[0, 2 * p:2 * p + 2, :, keys].reshape(LANES, pb).astype(BF16)
            carry, acc = _b_block(_dot(qs, k_t), v_t, tri_past, carry, acc, None, nq, v_transposed=True)
        o_ref[0, :, A_WIDTH + p * LANES:A_WIDTH + (p + 1) * LANES] = _b_finish(acc, gb_ref[...]).astype(o_ref.dtype)


def _attn_sample(lam_p, ga, gb2, q, kv, cak, cav, cbk, cbv):
    b, t, _ = q.shape
    past = cak.shape[1] // A_HEADS
    small = lambda shape: pl.BlockSpec(shape, lambda bi: (0, 0))
    per_b = lambda rows, width: pl.BlockSpec((1, rows, width), lambda bi: (bi, 0, 0))
    a_cache = per_b(past * A_HEADS, LANES)
    b_cache = pl.BlockSpec((1, B_HEADS, B_DIM, past), lambda bi: (bi, 0, 0, 0))
    return pl.pallas_call(
        _attn_sample_kernel,
        out_shape=jax.ShapeDtypeStruct((b, t, D_MODEL), BF16),
        grid=(b,),
        in_specs=[small((4, A_QK_DIM)), small((1, LANES)), small((1, LANES)),
                  per_b(t, D_MODEL), per_b(t, 2048), a_cache, a_cache, b_cache, b_cache],
        out_specs=per_b(t, D_MODEL),
        compiler_params=_params("arbitrary"),
        name="attn_sample",
    )(lam_p, ga, gb2, q, kv, cak, cav, cbk, cbv)


def _route(logits_t, bias_col):
    n = logits_t.shape[1]
    scores = jax.nn.sigmoid(logits_t)
    sel3 = (scores + bias_col).reshape(N_GROUPS, 8, n)
    eio = lax.broadcasted_iota(jnp.int32, (N_GROUPS, 8, n), 1)
    m1 = jnp.max(sel3, axis=1, keepdims=True)
    i1 = jnp.min(jnp.where(sel3 == m1, eio, 8), axis=1, keepdims=True)
    m2 = jnp.max(jnp.where(eio == i1, -jnp.inf, sel3), axis=1, keepdims=True)
    gscore = (m1 + m2).reshape(N_GROUPS, n)
    gio = lax.broadcasted_iota(jnp.int32, (N_GROUPS, n), 0)
    keep = jnp.zeros((N_GROUPS, n), F32)
    for _ in range(TOPK_GROUPS):
        g = jnp.max(gscore, axis=0, keepdims=True)
        gi = jnp.min(jnp.where(gscore == g, gio, N_GROUPS), axis=0, keepdims=True)
        hit = gio == gi
        keep = jnp.where(hit, 1.0, keep)
        gscore = jnp.where(hit, -jnp.inf, gscore)
    cur = jnp.where(keep.reshape(N_GROUPS, 1, n) > 0.0, sel3, NEG_INF).reshape(N_EXPERTS, n)
    xio = lax.broadcasted_iota(jnp.int32, (N_EXPERTS, n), 0)
    chosen = jnp.zeros((N_EXPERTS, n), F32)
    ids, raw = [], []
    for _ in range(TOP_K):
        mx = jnp.max(cur, axis=0, keepdims=True)
        ei = jnp.min(jnp.where(cur == mx, xio, N_EXPERTS), axis=0, keepdims=True)
        hit = xio == ei
        chosen = jnp.where(hit, 1.0, chosen)
        cur = jnp.where(hit, -jnp.inf, cur)
        ids.append(ei.astype(F32))
        raw.append(jnp.sum(jnp.where(hit, scores, 0.0), axis=0, keepdims=True))
    denom = raw[0]
    for r in raw[1:]:
        denom = denom + r
    wts = [r / denom * ROUTED_SCALE for r in raw]
    pad = jnp.zeros((8 - TOP_K, n), F32)
    return jnp.concatenate(ids + [pad] + wts + [pad], axis=0), chosen


def _outproj_kernel(o_ref, x_ref, g1_ref, sc_ref, sh_ref, wout_ref, wrt_ref, br_ref, x1_ref, h2_ref, rt_ref, cnt_ref):
    bb, tt, d = x_ref.shape
    n = bb * tt
    proj = _dot(o_ref[...].reshape(n, d), wout_ref[...]).reshape(bb, tt, d)
    x1 = x_ref[...] + g1_ref[...] * proj
    x1_ref[...] = x1
    h2 = (_rms(x1) * (1.0 + sc_ref[...]) + sh_ref[...]).reshape(n, d)
    h2hi = h2.astype(BF16)
    h2_ref[...] = h2hi.reshape(bb, tt, d)
    h2lo = (h2 - h2hi.astype(F32)).astype(BF16)
    w = wrt_ref[...]
    whi = w.astype(BF16)
    wlo = (w - whi.astype(F32)).astype(BF16)
    logits_t = _dot_nt(whi, h2hi) + _dot_nt(whi, h2lo) + _dot_nt(wlo, h2hi)
    rt, chosen = _route(logits_t, br_ref[...])
    rt_ref[...] = rt
    ones = jnp.ones((8, TM), BF16)
    for j in range(n // TM):
        cnt_ref[j] = _dot_nt(ones, chosen[:, j * TM:(j + 1) * TM].astype(BF16))


def _outproj(o, x, g1, sc, sh, wout_bf16, wr_t, br_col, bb, tt):
    b, t, d = x.shape
    n = bb * tt
    steps_t = t // tt
    tok = pl.BlockSpec((bb, tt, d), lambda i, j: (i, j, 0))
    mspec = pl.BlockSpec((bb, 1, d), lambda i, j: (i, 0, 0))
    full = lambda shape: pl.BlockSpec(shape, lambda i, j: (0, 0))
    return pl.pallas_call(
        _outproj_kernel,
        out_shape=(jax.ShapeDtypeStruct((b, t, d), F32), jax.ShapeDtypeStruct((b, t, d), BF16),
                   jax.ShapeDtypeStruct((16, b * t), F32),
                   jax.ShapeDtypeStruct((b * t // TM, 8, N_EXPERTS), F32)),
        grid=(b // bb, steps_t),
        in_specs=[tok, tok, mspec, mspec, mspec, full((d, d)), full((N_EXPERTS, d)), full((N_EXPERTS, 1))],
        out_specs=(tok, tok,
                   pl.BlockSpec((16, n), lambda i, j: (0, i * steps_t + j)),
                   pl.BlockSpec((n // TM, 8, N_EXPERTS), lambda i, j: (i * steps_t + j, 0, 0))),
        compiler_params=_params("arbitrary", "arbitrary"),
        name="outproj_router",
    )(o, x, g1, sc, sh, wout_bf16, wr_t, br_col)


FAST_BIT = 1
EXPERT_UNROLL = 4


def _seg_chunks(seg_ref, np_ref, base_ref, anchor, g, cap, visit):
    def body(e4, c):
        def chunk(seg, b):
            s0, n16, b0 = seg
            off = (n16 & ((1 << b) - 1)) * SEG

            @pl.when(((n16 >> b) & 1) == 1)
            def _():
                visit(pl.multiple_of(s0 + off, SEG), pl.multiple_of(b0 + off, SEG), SEG << b)

        segs = []
        for u in range(EXPERT_UNROLL):
            e = e4 * EXPERT_UNROLL + u
            idx = g * N_EXPERTS + e
            segs.append((seg_ref[idx], np_ref[idx], e * cap + base_ref[idx]))
        for s0, n16, b0 in segs:
            @pl.when(n16 == (1 << FAST_BIT))
            def _(s0=s0, b0=b0):
                visit(pl.multiple_of(s0, SEG), pl.multiple_of(b0, SEG), SEG << FAST_BIT)
        for seg in segs:
            @pl.when(jnp.logical_and(seg[1] != (1 << FAST_BIT), seg[1] != 0))
            def _(seg=seg):
                anchor[...] = jnp.zeros_like(anchor)
                for b in range(SEG_BITS):
                    chunk(seg, b)
        return c
    lax.fori_loop(0, N_EXPERTS // EXPERT_UNROLL, body, 0)


def _tile_rows(seg_ref, np_ref, g):
    last = g * N_EXPERTS + N_EXPERTS - 1
    return seg_ref[last] + np_ref[last] * SEG


def _wait_tile(seg_ref, np_ref, g, wait_rows):
    n16 = _tile_rows(seg_ref, np_ref, g) >> SEG_SHIFT
    for b in range(TILE_BITS):
        @pl.when(((n16 >> b) & 1) == 1)
        def _(b=b):
            wait_rows(SEG << b)


def _dispatch_kernel(seg_ref, np_ref, base_ref, fin_ref, rem_ref, rt_ref, h2_ref, *rest, t0, nsteps, cap, fill):
    xs_ref, pos_ref, buf, zbuf, anchor, sem, zsem = rest[-7:]
    s = pl.program_id(0)
    g = t0 + s
    slot = s % 2

    def copy(slot_, row, hrow, size):
        return pltpu.make_async_copy(buf.at[slot_, pl.ds(row, size)], xs_ref.at[pl.ds(hrow, size)], sem.at[slot_])

    def chunks(g_, fn):
        _seg_chunks(seg_ref, np_ref, base_ref, anchor, g_, cap, fn)

    def wait_tile(g_, slot_):
        _wait_tile(seg_ref, np_ref, g_, lambda n: copy(slot_, 0, 0, n).wait())

    @pl.when(s >= 2)
    def _():
        wait_tile(g - 2, slot)

    rt = rt_ref[...]
    eio = lax.broadcasted_iota(jnp.int32, (N_EXPERTS, TM), 0).astype(F32)
    chosen = jnp.zeros((N_EXPERTS, TM), F32)
    for k in range(TOP_K):
        chosen = jnp.where(eio == rt[k:k + 1], 1.0, chosen)
    t_row = lax.broadcasted_iota(jnp.int32, (TM, TM), 0)
    t_col = lax.broadcasted_iota(jnp.int32, (TM, TM), 1)
    rank = _dot(chosen.astype(BF16), jnp.where(t_row < t_col, 1.0, 0.0).astype(BF16))
    n_e = jnp.sum(chosen, axis=1, keepdims=True).astype(jnp.int32)
    np16 = ((n_e + (SEG - 1)) >> SEG_SHIFT).astype(F32)
    e_row = lax.broadcasted_iota(jnp.int32, (N_EXPERTS, N_EXPERTS), 0)
    e_col = lax.broadcasted_iota(jnp.int32, (N_EXPERTS, N_EXPERTS), 1)
    lower = jnp.where(e_col < e_row, 1.0, 0.0).astype(BF16)
    seg0 = SEG * _dot(lower, jnp.broadcast_to(np16, (N_EXPERTS, LANES)).astype(BF16))[:, 0:1]
    pos = seg0 + rank
    pos_rows = [jnp.sum(jnp.where(eio == rt[k:k + 1], pos, 0.0), axis=0, keepdims=True) for k in range(TOP_K)]
    pos_ref[...] = jnp.concatenate(pos_rows + [jnp.zeros((8 - TOP_K, TM), F32)], axis=0)

    h2 = h2_ref[...]
    npairs = (_tile_rows(seg_ref, np_ref, g) + (2 * CH - 1)) >> (CH_SHIFT + 1)

    def chunk_pair(c2, carry):
        for half in range(2):
            r0 = pl.multiple_of(c2 * (2 * CH) + half * CH, CH)
            rio = (lax.broadcasted_iota(jnp.int32, (CH, TM), 0) + r0).astype(F32)
            p = jnp.zeros((CH, TM), F32)
            for k in range(TOP_K):
                p = jnp.where(rio == pos_rows[k], 1.0, p)
            buf[slot, pl.ds(r0, CH), :] = _dot(p.astype(BF16), h2).astype(BF16)
        return carry
    lax.fori_loop(0, npairs, chunk_pair, 0)

    chunks(g, lambda r, hr, n: copy(slot, r, hr, n).start())

    @pl.when(s == nsteps - 1)
    def _():
        if nsteps >= 2:
            wait_tile(g - 1, 1 - slot)
        wait_tile(g, slot)
        if fill:
            zbuf[...] = jnp.zeros_like(zbuf)

            def zero_chunks(fn):
                def body(e, c):
                    r16 = rem_ref[e]
                    h0 = e * cap + fin_ref[e]
                    for b in range(FILL_BITS):
                        off = (r16 & ((1 << b) - 1)) * SEG

                        @pl.when(((r16 >> b) & 1) == 1)
                        def _(b=b, off=off):
                            size = SEG << b
                            fn(pltpu.make_async_copy(zbuf.at[pl.ds(0, size)],
                                                     xs_ref.at[pl.ds(pl.multiple_of(h0 + off, SEG), size)], zsem))
                    return c
                lax.fori_loop(0, N_EXPERTS, body, 0)
            zero_chunks(lambda cp: cp.start())
            zero_chunks(lambda cp: cp.wait())


def _dispatch(plan, rt, h2, xs, t0, cap, fill):
    n, d = h2.shape
    nsteps = n // TM
    kern = functools.partial(_dispatch_kernel, t0=t0, nsteps=nsteps, cap=cap, fill=fill)
    in_specs = [pl.BlockSpec((16, TM), lambda s, *_: (0, s)), pl.BlockSpec((TM, d), lambda s, *_: (s, 0))]
    args = [rt, h2]
    aliases = {}
    if xs is not None:
        in_specs.append(pl.BlockSpec(memory_space=pl.ANY))
        args.append(xs)
        aliases = {len(plan) + 2: 0}
    return pl.pallas_call(
        kern,
        out_shape=(jax.ShapeDtypeStruct((N_EXPERTS * cap, d), BF16), jax.ShapeDtypeStruct((8, n), F32)),
        grid_spec=pltpu.PrefetchScalarGridSpec(
            num_scalar_prefetch=len(plan), grid=(nsteps,), in_specs=in_specs,
            out_specs=(pl.BlockSpec(memory_space=pl.ANY), pl.BlockSpec((8, TM), lambda s, *_: (0, s))),
            scratch_shapes=[pltpu.VMEM((2, RMAX, d), BF16), pltpu.VMEM((SEG << (FILL_BITS - 1), d), BF16),
                            pltpu.VMEM((8, LANES), F32),
                            pltpu.SemaphoreType.DMA((2,)), pltpu.SemaphoreType.DMA(())]),
        input_output_aliases=aliases,
        compiler_params=_params("arbitrary"),
        name="dispatch",
    )(*plan, *args)


def _expert_kernel(te_ref, tb_ref, tv_ref, tf_ref, xs_ref, wg_ref, wu_ref, wd_ref, ys_ref, wg16, wu16, wd16):
    i = pl.program_id(0)

    @pl.when(tf_ref[i] == 1)
    def _():
        wg16[...] = wg_ref[0].astype(BF16)
        wu16[...] = wu_ref[0].astype(BF16)
        wd16[...] = wd_ref[0].astype(BF16)

    @pl.when(tv_ref[i] == 1)
    def _():
        x = xs_ref[...]
        act = _silu(_dot(x, wg16[...])) * _dot(x, wu16[...])
        ys_ref[...] = _dot(act.astype(BF16), wd16[...]).astype(BF16)


def _experts(sched, xs, wg, wu, wd, max_tiles):
    d = xs.shape[1]
    wspec = lambda shape: pl.BlockSpec((1,) + shape, lambda i, te, tb, tv, tf: (te[i], 0, 0))
    row_map = lambda i, te, tb, tv, tf: (tb[i], 0)
    return pl.pallas_call(
        _expert_kernel,
        out_shape=jax.ShapeDtypeStruct(xs.shape, BF16),
        grid_spec=pltpu.PrefetchScalarGridSpec(
            num_scalar_prefetch=4, grid=(max_tiles,),
            in_specs=[pl.BlockSpec((TR, d), row_map),
                      wspec((d, D_EXPERT)), wspec((d, D_EXPERT)), wspec((D_EXPERT, d))],
            out_specs=pl.BlockSpec((TR, d), row_map),
            scratch_shapes=[pltpu.VMEM((d, D_EXPERT), BF16), pltpu.VMEM((d, D_EXPERT), BF16),
                            pltpu.VMEM((D_EXPERT, d), BF16)]),
        compiler_params=_params("arbitrary"),
        name="experts",
    )(*sched, xs, wg, wu, wd)


def _combine_kernel(seg_ref, np_ref, base_ref, pos_ref, rt_ref, x1_ref, h2_ref, g2_ref, wsg_ref, wsu_ref, wsd_ref,
                    gf_ref, ys_ref, y_ref, ybuf, acc, anchor, sem, *, t0, nsteps, cap):
    bb, tt, d = x1_ref.shape
    s = pl.program_id(0)
    g = t0 + s
    slot = s % 2

    def copy(slot_, row, hrow, size):
        return pltpu.make_async_copy(ys_ref.at[pl.ds(hrow, size)], ybuf.at[slot_, pl.ds(row, size)], sem.at[slot_])

    def chunks(g_, fn):
        _seg_chunks(seg_ref, np_ref, base_ref, anchor, g_, cap, fn)

    @pl.when(s == 0)
    def _():
        ybuf[...] = jnp.zeros_like(ybuf)
        chunks(g, lambda r, hr, n: copy(0, r, hr, n).start())

    @pl.when(s + 1 < nsteps)
    def _():
        chunks(g + 1, lambda r, hr, n: copy(1 - slot, r, hr, n).start())

    _wait_tile(seg_ref, np_ref, g, lambda n: copy(slot, 0, 0, n).wait())

    stacked = jnp.concatenate([pos_ref[...], jnp.zeros((8, TM), F32), rt_ref[...],
                               jnp.zeros((LANES - 32, TM), F32)], axis=0)
    cols = stacked.T
    pos_b = [jnp.broadcast_to(cols[:, k:k + 1], (TM, LANES)) for k in range(TOP_K)]
    wt_b = [jnp.broadcast_to(cols[:, 24 + k:25 + k], (TM, LANES)) for k in range(TOP_K)]
    npairs = (_tile_rows(seg_ref, np_ref, g) + (2 * CH - 1)) >> (CH_SHIFT + 1)
    acc[...] = jnp.zeros_like(acc)

    def chunk_pair(c2, carry):
        part = None
        for half in range(2):
            r0 = pl.multiple_of(c2 * (2 * CH) + half * CH, CH)
            w_parts = []
            for lane0 in range(0, CH, LANES):
                rio = (lax.broadcasted_iota(jnp.int32, (TM, LANES), 1) + (r0 + lane0)).astype(F32)
                w = jnp.zeros((TM, LANES), F32)
                for k in range(TOP_K):
                    w = jnp.where(rio == pos_b[k], wt_b[k], w)
                w_parts.append(w.astype(BF16))
            prod = _dot(jnp.concatenate(w_parts, axis=1), ybuf[slot, pl.ds(r0, CH), :])
            part = prod if part is None else part + prod
        acc[...] += part
        return carry
    lax.fori_loop(0, npairs, chunk_pair, 0)

    h = h2_ref[...].reshape(bb * tt, d)
    act = _silu(_dot(h, wsg_ref[...])) * _dot(h, wsu_ref[...])
    moe = acc[...] + _dot(act.astype(BF16), wsd_ref[...])
    x2 = x1_ref[...] + g2_ref[...] * moe.reshape(bb, tt, d)
    y_ref[...] = _rms(x2) * gf_ref[...]


def _combine(plan3, pos, rt, x1, h2, g2, wsg, wsu, wsd, gf, ys, bb, tt, t0, cap):
    b, t, d = x1.shape
    steps_t = t // tt
    nsteps = (b // bb) * steps_t
    bidx = lambda s: (s // steps_t, s % steps_t, 0)
    tok = pl.BlockSpec((bb, tt, d), lambda s, *_: bidx(s))
    full = lambda shape: pl.BlockSpec(shape, lambda s, *_: (0,) * len(shape))
    kern = functools.partial(_combine_kernel, t0=t0, nsteps=nsteps, cap=cap)
    return pl.pallas_call(
        kern,
        out_shape=jax.ShapeDtypeStruct((b, t, d), F32),
        grid_spec=pltpu.PrefetchScalarGridSpec(
            num_scalar_prefetch=3, grid=(nsteps,),
            in_specs=[pl.BlockSpec((8, TM), lambda s, *_: (0, s)), pl.BlockSpec((16, TM), lambda s, *_: (0, s)),
                      tok, tok, pl.BlockSpec((bb, 1, d), lambda s, *_: (s // steps_t, 0, 0)),
                      full(wsg.shape), full(wsu.shape), full(wsd.shape), full((1, 1, d)),
                      pl.BlockSpec(memory_space=pl.ANY)],
            out_specs=tok,
            scratch_shapes=[pltpu.VMEM((2, RMAX, d), BF16), pltpu.VMEM((TM, d), F32),
                            pltpu.VMEM((8, LANES), F32), pltpu.SemaphoreType.DMA((2,))]),
        compiler_params=_params("arbitrary"),
        name="combine",
    )(*plan3, pos, rt, x1, h2, g2, wsg, wsu, wsd, gf, ys)


def _plan(cnt, cap, max_tiles):
    np16 = (cnt + (SEG - 1)) // SEG
    npad = np16 * SEG
    seg = jnp.cumsum(npad, axis=1) - npad
    base = jnp.cumsum(npad, axis=0) - npad
    fin = jnp.sum(npad, axis=0)
    rem16 = ((-fin) % TR) // SEG
    tiles_e = (fin + (TR - 1)) // TR
    ends = jnp.cumsum(tiles_e)
    i = jnp.arange(max_tiles, dtype=jnp.int32)
    valid = i < ends[-1]
    ii = jnp.where(valid, i, jnp.maximum(ends[-1] - 1, 0))
    te = jnp.sum((ends[None, :] <= ii[:, None]).astype(jnp.int32), axis=1)
    te = jnp.minimum(te, N_EXPERTS - 1)
    owner = jnp.arange(N_EXPERTS, dtype=jnp.int32)[None, :] == te[:, None]
    r = ii - jnp.sum(jnp.where(owner, (ends - tiles_e)[None, :], 0), axis=1)
    tb = te * (cap // TR) + r
    first = jnp.logical_and(valid, r == 0)
    i32 = lambda a: a.astype(jnp.int32).reshape(-1)
    return (i32(seg), i32(np16), i32(base), i32(fin), i32(rem16)), (te, i32(tb), i32(valid), i32(first))


def kernel(x_prompt, x_sample, cache_a_k, cache_a_v, cache_b_k, cache_b_v, c_prompt, c_sample, w_ada, b_ada, w_in, lam_params, g_sub_a, g_sub_b, w_out, w_router, b_router, w_gate, w_up, w_down, w_shared_gate, w_shared_up, w_shared_down, g_final):
    bp, tp, d = x_prompt.shape
    bs, ts, _ = x_sample.shape
    past = cache_a_k.shape[2]

    mod = _ada(jnp.concatenate([c_prompt, c_sample], axis=0), w_ada[0], b_ada[0])
    mod = mod.reshape(bp + bs, 6, 1, d)
    sh1, sc1, g1, sh2, sc2, g2 = (mod[:, k] for k in range(6))

    w_in16 = w_in[0].astype(BF16)
    w_out16 = w_out[0].astype(BF16)
    wr_t = w_router[0].T
    br_col = b_router[0].reshape(N_EXPERTS, 1)
    wsg16, wsu16, wsd16 = (w_shared_gate[0].astype(BF16), w_shared_up[0].astype(BF16),
                           w_shared_down[0].astype(BF16))
    ga = g_sub_a[0].reshape(1, LANES)
    gb2 = jnp.concatenate([g_sub_b[0], g_sub_b[0]]).reshape(1, LANES)
    gf = g_final.reshape(1, 1, d)
    lam_p = lam_params[0]

    def front(x, sl, bb, tt, attn):
        q, ka, va, kb, vb, kv = _inproj(x, sc1[sl], sh1[sl], w_in16, bb, tt)
        o = attn(q, kv)
        x1, h2, rt, cnt = _outproj(o, x, g1[sl], sc2[sl], sh2[sl], w_out16, wr_t, br_col, bb, tt)
        return (x1, h2, rt, cnt), (ka, va, kb, vb)

    mid_p, rows_p = front(x_prompt, slice(0, bp), 1, 512,
                          lambda q, kv: _attn_prompt(lam_p, ga, gb2, q, kv, 512))
    caches = (cache_a_k.reshape(bs, past * A_HEADS, LANES), cache_a_v.reshape(bs, past * A_HEADS, LANES),
              jnp.transpose(cache_b_k[0], (0, 2, 3, 1)), jnp.transpose(cache_b_v[0], (0, 2, 3, 1)))
    mid_s, rows_s = front(x_sample, slice(bp, bp + bs), 8, ts,
                          lambda q, kv: _attn_sample(lam_p, ga, gb2, q, kv, *caches))

    (x1_p, h2_p, rt_p, cnt_p), (x1_s, h2_s, rt_s, cnt_s) = mid_p, mid_s
    n_p, n_s = bp * tp, bs * ts
    tiles_p, tiles_s = n_p // TM, n_s // TM
    tiles = tiles_p + tiles_s
    cap = -(-(n_p + n_s + tiles * (SEG - 1)) // TR) * TR
    max_tiles = (TOP_K * (n_p + n_s) + tiles * N_EXPERTS * (SEG - 1)) // TR + N_EXPERTS
    cnt = jnp.concatenate([cnt_p[:, 0, :], cnt_s[:, 0, :]], axis=0).astype(jnp.int32)
    plan, sched = _plan(cnt, cap, max_tiles)
    xs, pos_p = _dispatch(plan, rt_p, h2_p.reshape(n_p, d), None, 0, cap, False)
    xs, pos_s = _dispatch(plan, rt_s, h2_s.reshape(n_s, d), xs, tiles_p, cap, True)
    ys = _experts(sched, xs, w_gate[0], w_up[0], w_down[0], max_tiles)
    y_p = _combine(plan[:3], pos_p, rt_p, x1_p, h2_p, g2[0:bp], wsg16, wsu16, wsd16, gf, ys,
                   1, TM, 0, cap)
    y_s = _combine(plan[:3], pos_s, rt_s, x1_s, h2_s, g2[bp:bp + bs], wsg16, wsu16, wsd16, gf, ys,
                   TM // ts, ts, tiles_p, cap)

    def shape_rows(rows, b, t):
        ka, va, kb, vb = rows
        return (ka.reshape(1, b, t, A_HEADS, 2 * A_QK_DIM), va.reshape(1, b, t, A_HEADS, 2 * A_QK_DIM),
                kb.reshape(1, b, t, B_HEADS, B_DIM), vb.reshape(1, b, t, B_HEADS, B_DIM))

    return (y_p, y_s) + shape_rows(rows_p, bp, tp) + shape_rows(rows_s, bs, ts)
```
